```python
import jax, jax.numpy as jnp
from jax import lax
import numpy as np

D_MODEL = 1024
BATCH = 16
SEQ = 256
DEPTH = 2
DEC_BATCH = 8
DEC_SEQ = 2048
PAST_LEN = 512

GRID_W = 64
CONV_W = D_MODEL // 2
DN_HEADS = 4
DK = 128
DV = 128
QK_W = DN_HEADS * DK
V_W = DN_HEADS * DV
D_MIX = CONV_W + V_W
D_FF = ((8 * D_MODEL // 3 + 127) // 128) * 128
CHUNK = 64
EPS = 1e-6
IN_COLS = 3 * CONV_W + 2 * QK_W + 2 * V_W + 4 * DN_HEADS

kernel_name = "hybrid_conv_gdn_diffusion_step"


def rmsnorm(x, w):
    xf = x.astype(jnp.float32)
    y = xf * lax.rsqrt(jnp.mean(xf * xf, axis=-1, keepdims=True) + EPS)
    return (y * w.astype(jnp.float32)).astype(x.dtype)


def l2norm(x):
    return x * lax.rsqrt(jnp.sum(x * x, axis=-1, keepdims=True) + EPS)


def row_conv3(u, w, row_len):
    b, l, ch = u.shape
    rows = l // row_len
    u4 = u.reshape(b, rows, row_len, ch)
    up = jnp.pad(u4, ((0, 0), (0, 0), (1, 1), (0, 0)))
    y = w[0] * up[:, :, :-2] + w[1] * up[:, :, 1:-1] + w[2] * up[:, :, 2:]
    return y.reshape(b, l, ch)


def grid_conv3x3(u, w, row_len):
    b, l, ch = u.shape
    rows = l // row_len
    u4 = u.reshape(b, rows, row_len, ch)
    y = lax.conv_general_dilated(u4, w[:, :, None, :], (1, 1), 'SAME',
                                 dimension_numbers=('NHWC', 'HWIO', 'NHWC'),
                                 feature_group_count=ch)
    return y.reshape(b, l, ch)


def chunked_gated_delta(q, k, v, beta, log_alpha, s0):
    b, l, h, dk = q.shape
    dv = v.shape[-1]
    n = l // CHUNK

    def blk(t):
        return jnp.moveaxis(t.reshape((b, n, CHUNK, h) + t.shape[3:]), 3, 1)

    q, k, v, beta = blk(q), blk(k), blk(v), blk(beta)
    g = jnp.cumsum(blk(log_alpha), axis=-1)
    idx = jnp.arange(CHUNK)
    strict = idx[:, None] > idx[None, :]
    incl = idx[:, None] >= idx[None, :]
    diff = g[..., :, None] - g[..., None, :]
    dec_strict = jnp.where(strict, jnp.exp(jnp.where(strict, diff, 0.0)), 0.0)
    dec_incl = jnp.where(incl, jnp.exp(jnp.where(incl, diff, 0.0)), 0.0)
    kb = k * beta[..., None]
    vb = v * beta[..., None]
    a_mat = jnp.eye(CHUNK, dtype=jnp.float32) + jnp.einsum('bhnid,bhnjd->bhnij', kb, k) * dec_strict
    u_val = lax.linalg.triangular_solve(a_mat, vb, left_side=True, lower=True, unit_diagonal=True)
    w_key = lax.linalg.triangular_solve(a_mat, kb * jnp.exp(g)[..., None], left_side=True,
                                        lower=True, unit_diagonal=True)
    qk = jnp.einsum('bhnid,bhnjd->bhnij', q, k) * dec_incl
    q_dec = q * jnp.exp(g)[..., None]
    g_last = g[..., -1:]
    k_end = k * jnp.exp(g_last - g)[..., None]
    chunk_decay = jnp.exp(g_last)[..., None]
    xs = tuple(jnp.moveaxis(t, 2, 0) for t in (u_val, w_key, qk, q_dec, k_end, chunk_decay))

    def step(s, inp):
        u_c, w_c, qk_c, qd_c, ke_c, dl_c = inp
        v_new = u_c - jnp.einsum('bhcd,bhde->bhce', w_c, s)
        o_c = jnp.einsum('bhcd,bhde->bhce', qd_c, s) + jnp.einsum('bhij,bhje->bhie', qk_c, v_new)
        s = s * dl_c + jnp.einsum('bhcd,bhce->bhde', ke_c, v_new)
        return s, o_c

    s_final, o = lax.scan(step, s0.astype(jnp.float32), xs)
    o = jnp.moveaxis(jnp.moveaxis(o, 0, 2), 1, 3).reshape(b, l, h, dv)
    return o, s_final


def mixer(h, row_len, s0_f, s0_b, w_in, conv_a_w, qkv_conv_w, a_log, dt_bias, o_norm_w, w_out):
    b, l, _ = h.shape
    z = h @ w_in
    cuts = [CONV_W, 2 * CONV_W, 3 * CONV_W, 3 * CONV_W + 2 * QK_W + V_W,
            3 * CONV_W + 2 * QK_W + 2 * V_W, 3 * CONV_W + 2 * QK_W + 2 * V_W + 2 * DN_HEADS]
    cb, cc, cx, qkv, gate, b_logit, a_in = jnp.split(z, cuts, axis=-1)
    y_conv = cb * row_conv3(cc * cx, conv_a_w, row_len)
    qkv = jax.nn.silu(row_conv3(qkv, qkv_conv_w, row_len)).astype(jnp.float32)
    q, k, v = jnp.split(qkv, [QK_W, 2 * QK_W], axis=-1)
    q = l2norm(q.reshape(b, l, DN_HEADS, DK)) * (DK ** -0.5)
    k = l2norm(k.reshape(b, l, DN_HEADS, DK))
    v = v.reshape(b, l, DN_HEADS, DV)
    beta = jax.nn.sigmoid(b_logit.astype(jnp.float32)).reshape(b, l, 2, DN_HEADS)
    log_alpha = -jnp.exp(a_log.astype(jnp.float32)) * jax.nn.softplus(
        a_in.astype(jnp.float32).reshape(b, l, 2, DN_HEADS) + dt_bias.astype(jnp.float32))
    o_f, s_f = chunked_gated_delta(q, k, v, beta[:, :, 0], log_alpha[:, :, 0], s0_f)
    rev = lambda t: t[:, ::-1]
    o_b, s_b = chunked_gated_delta(rev(q), rev(k), rev(v), rev(beta[:, :, 1]),
                                   rev(log_alpha[:, :, 1]), s0_b)
    o = o_f + rev(o_b)
    o = rmsnorm(o, o_norm_w) * jax.nn.silu(gate.astype(jnp.float32).reshape(b, l, DN_HEADS, DV))
    y = jnp.concatenate([y_conv, o.reshape(b, l, V_W).astype(h.dtype)], axis=-1) @ w_out
    return y, s_f, s_b


def conv_glu(h, row_len, w_up, ffn_conv_w, w_down):
    u, gt = jnp.split(h @ w_up, 2, axis=-1)
    gt = grid_conv3x3(gt, ffn_conv_w, row_len)
    return (jax.nn.gelu(gt) * u) @ w_down


def block(x, mod, row_len, s0_f, s0_b, norm1_w, w_in, conv_a_w, qkv_conv_w, a_log, dt_bias,
          o_norm_w, w_out, norm2_w, w_up, ffn_conv_w, w_down):
    shift1, scale1, gate1, shift2, scale2, gate2 = jnp.split(mod, 6, axis=-1)
    h = rmsnorm(x, norm1_w) * (1 + scale1) + shift1
    y, s_f, s_b = mixer(h, row_len, s0_f, s0_b, w_in, conv_a_w, qkv_conv_w, a_log, dt_bias,
                        o_norm_w, w_out)
    x = x + gate1 * y
    h = rmsnorm(x, norm2_w) * (1 + scale2) + shift2
    x = x + gate2 * conv_glu(h, row_len, w_up, ffn_conv_w, w_down)
    return x, s_f, s_b


def setup_inputs(seed: int = 0) -> dict:
    key = jax.random.key(seed)
    ks = jax.random.split(key, 24)
    nrm = lambda k, shape, s: jax.random.normal(k, shape, jnp.float32) * s
    dt = jnp.exp(jax.random.uniform(ks[10], (DEPTH, 2, DN_HEADS), jnp.float32,
                                    np.log(1e-3), np.log(1e-1)))
    return {
        "x_prompt": nrm(ks[0], (BATCH, SEQ, D_MODEL), 1.0),
        "x_sample": nrm(ks[1], (DEC_BATCH, DEC_SEQ, D_MODEL), 1.0),
        "state_deltanet": nrm(ks[2], (DEC_BATCH, DEPTH, 2, DN_HEADS, DK, DV), 0.3),
        "c": nrm(ks[3], (DEC_BATCH, D_MODEL), 1.0),
        "c_ctx": nrm(ks[4], (D_MODEL,), 1.0),
        "w_ada": nrm(ks[5], (DEPTH, D_MODEL, 6 * D_MODEL), 0.5 * D_MODEL ** -0.5),
        "b_ada": nrm(ks[6], (DEPTH, 6 * D_MODEL), 0.02),
        "norm1_w": 1.0 + nrm(ks[7], (DEPTH, D_MODEL), 0.02),
        "w_in": nrm(ks[8], (DEPTH, D_MODEL, IN_COLS), D_MODEL ** -0.5),
        "conv_a_w": nrm(ks[9], (DEPTH, 3, CONV_W), 0.5),
        "qkv_conv_w": nrm(ks[11], (DEPTH, 3, 2 * QK_W + V_W), 0.5),
        "a_log": jnp.log(jax.random.uniform(ks[12], (DEPTH, 2, DN_HEADS), jnp.float32, 1.0, 16.0)),
        "dt_bias": dt + jnp.log(-jnp.expm1(-dt)),
        "o_norm_w": 1.0 + nrm(ks[13], (DEPTH, DV), 0.02),
        "w_out": nrm(ks[14], (DEPTH, D_MIX, D_MODEL), D_MIX ** -0.5),
        "norm2_w": 1.0 + nrm(ks[15], (DEPTH, D_MODEL), 0.02),
        "w_up": nrm(ks[16], (DEPTH, D_MODEL, 2 * D_FF), D_MODEL ** -0.5),
        "ffn_conv_w": nrm(ks[17], (DEPTH, 3, 3, D_FF), 1.0 / 3.0),
        "w_down": nrm(ks[18], (DEPTH, D_FF, D_MODEL), D_FF ** -0.5),
        "final_norm_w": 1.0 + nrm(ks[19], (D_MODEL,), 0.02),
    }


def reference(x_prompt, x_sample, state_deltanet, c, c_ctx, w_ada, b_ada, norm1_w, w_in,
              conv_a_w, qkv_conv_w, a_log, dt_bias, o_norm_w, w_out, norm2_w, w_up,
              ffn_conv_w, w_down, final_norm_w):
    b_ctx, ctx_len, _ = x_prompt.shape
    xp = x_prompt
    xs = x_sample
    zero_state = jnp.zeros((b_ctx, DN_HEADS, DK, DV), jnp.float32)
    ctx_states = []
    for l in range(DEPTH):
        layer_w = (norm1_w[l], w_in[l], conv_a_w[l], qkv_conv_w[l], a_log[l], dt_bias[l],
                   o_norm_w[l], w_out[l], norm2_w[l], w_up[l], ffn_conv_w[l], w_down[l])
        mod_ctx = jax.nn.silu(c_ctx) @ w_ada[l] + b_ada[l]
        xp, s_f, s_b = block(xp, mod_ctx, ctx_len, zero_state, zero_state, *layer_w)
        ctx_states.append(jnp.stack([s_f, s_b], axis=1))
        mod_lat = (jax.nn.silu(c) @ w_ada[l] + b_ada[l])[:, None, :]
        xs, _, _ = block(xs, mod_lat, GRID_W, state_deltanet[:, l, 0], state_deltanet[:, l, 1],
                         *layer_w)
    y_prompt = rmsnorm(xp, final_norm_w)
    y_sample = rmsnorm(xs, final_norm_w)
    new_state_deltanet = jnp.stack(ctx_states, axis=1)
    return (y_prompt, y_sample, new_state_deltanet)
```

```python
import functools

import jax
import jax.numpy as jnp
from jax import lax
from jax.experimental import pallas as pl
from jax.experimental.pallas import tpu as pltpu

D_MODEL = 1024
DEPTH = 2
GRID_W = 64
CONV_W = D_MODEL // 2
DN_HEADS = 4
DK = 128
DV = 128
QK_W = DN_HEADS * DK
V_W = DN_HEADS * DV
D_FF = 2816
CHUNK = 64
EPS = 1e-6
MAIN_COLS = 3 * CONV_W + 2 * QK_W + 2 * V_W
N_GATE_COLS = 4 * DN_HEADS

LANES = 128
TM = 256
TB = 2048
FB = 256
MOD_ROWS = 16
VMEM_LIMIT = 56 * 1024 * 1024

BF16 = jnp.bfloat16
F32 = jnp.float32


def _dot(a, b):
    return jnp.dot(a, b, preferred_element_type=F32)


def _dot_nt(a, b):
    return lax.dot_general(a, b, (((1,), (1,)), ((), ())), preferred_element_type=F32)


def _dot_tn(a, b):
    return lax.dot_general(a, b, (((0,), (0,)), ((), ())), preferred_element_type=F32)


def _dot_split(a, b):
    ah = a.astype(BF16)
    al = (a - ah.astype(F32)).astype(BF16)
    bh = b.astype(BF16)
    bl = (b - bh.astype(F32)).astype(BF16)
    return _dot(ah, bh) + (_dot(ah, bl) + _dot(al, bh))


def _silu(x):
    return x * jax.nn.sigmoid(x)


def _mean_sq(x):
    return jnp.mean(x * x, axis=-1, keepdims=True)


def _params(*sem):
    return pltpu.CompilerParams(dimension_semantics=sem, vmem_limit_bytes=VMEM_LIMIT)


def _mod_kernel(c_ref, w_ref, b_ref, o_ref):
    a = _silu(c_ref[...]).astype(BF16)
    o_ref[0, 0] = _dot(a, w_ref[0].astype(BF16)) + b_ref[0]


def _modulation(c_all, w_ada, b_ada):
    return pl.pallas_call(
        _mod_kernel,
        grid=(DEPTH, 6),
        in_specs=[
            pl.BlockSpec((MOD_ROWS, D_MODEL), lambda l, n: (0, 0)),
            pl.BlockSpec((1, D_MODEL, D_MODEL), lambda l, n: (l, 0, n)),
            pl.BlockSpec((1, 1, D_MODEL), lambda l, n: (l, 0, n)),
        ],
        out_specs=pl.BlockSpec((1, 1, MOD_ROWS, D_MODEL), lambda l, n: (l, n, 0, 0)),
        out_shape=jax.ShapeDtypeStruct((DEPTH, 6, MOD_ROWS, D_MODEL), F32),
        compiler_params=_params("arbitrary", "arbitrary"),
        name="modulation",
    )(c_all, w_ada, b_ada.reshape(DEPTH, 1, 6 * D_MODEL))


def _mod_row(mod_ref, which, row):
    return mod_ref[0, which, pl.ds(row, 1), :]


def _conv3(u, w_ref, lo, first, last):
    n = u.shape[0]
    width = u.shape[1]
    up = jnp.where(first, 0.0, pltpu.roll(u, 1, 0))
    un = jnp.where(last, 0.0, pltpu.roll(u, n - 1, 0))
    w0 = w_ref[0, 0:1, lo:lo + width]
    w1 = w_ref[0, 1:2, lo:lo + width]
    w2 = w_ref[0, 2:3, lo:lo + width]
    return w0 * up + w1 * u + w2 * un


def _pre_kernel(row_len, tiles_per_batch, row0, has_ffn, *refs):
    if has_ffn:
        (x_ref, ffn_ref, modp_ref, mod_ref, n1_ref, wm_ref, ws_ref, cw_ref, qw_ref, gp_ref,
         xo_ref, yc_ref, q_ref, k_ref, v_ref, g_ref, sm_ref) = refs
    else:
        (x_ref, mod_ref, n1_ref, wm_ref, ws_ref, cw_ref, qw_ref, gp_ref,
         yc_ref, q_ref, k_ref, v_ref, g_ref, sm_ref) = refs
    row = row0 + pl.program_id(0) // tiles_per_batch if tiles_per_batch else row0
    x = x_ref[...]
    if has_ffn:
        x = x + _mod_row(modp_ref, 5, row) * ffn_ref[...]
        xo_ref[...] = x
    xn = x * lax.rsqrt(_mean_sq(x) + EPS)
    h = xn * (n1_ref[0] * (1.0 + _mod_row(mod_ref, 1, row))) + _mod_row(mod_ref, 0, row)
    hb = h.astype(BF16)

    pos = lax.broadcasted_iota(jnp.int32, (TM, 1), 0) & (row_len - 1)
    first = pos == 0
    last = pos == row_len - 1

    cb = _dot(hb, wm_ref[0, :, 0:CONV_W])
    cc = _dot(hb, wm_ref[0, :, CONV_W:2 * CONV_W])
    cx = _dot(hb, wm_ref[0, :, 2 * CONV_W:3 * CONV_W])
    yc_ref[...] = (cb * _conv3(cc * cx, cw_ref, 0, first, last)).astype(BF16)

    base = 3 * CONV_W
    for idx, out_ref in enumerate((q_ref, k_ref, v_ref)):
        z = _dot(hb, wm_ref[0, :, base + idx * QK_W: base + (idx + 1) * QK_W])
        a = _silu(_conv3(z, qw_ref, idx * QK_W, first, last))
        if idx < 2:
            scale = DK ** -0.5 if idx == 0 else 1.0
            for hd in range(DN_HEADS):
                ah = a[:, hd * DK:(hd + 1) * DK]
                ss = jnp.sum(ah * ah, axis=-1, keepdims=True)
                out_ref[:, hd * DK:(hd + 1) * DK] = ah * (lax.rsqrt(ss + EPS) * scale)
        else:
            out_ref[...] = a
    g_ref[...] = _silu(_dot(hb, wm_ref[0, :, base + 3 * QK_W: base + 3 * QK_W + V_W]))

    zs = _dot(hb, ws_ref[0])
    lane = lax.broadcasted_iota(jnp.int32, (TM, LANES), 1)
    a_log = gp_ref[0, 0:1, :]
    dt_bias = gp_ref[0, 1:2, :]
    log_alpha = -jnp.exp(a_log) * jax.nn.softplus(zs + dt_bias)
    sm_ref[...] = jnp.where(lane < 2 * DN_HEADS, jax.nn.sigmoid(zs),
                            jnp.where(lane < N_GATE_COLS, log_alpha, 0.0))


def _pre_mixer(layer, x, ffn, mod, row_len, tiles_per_batch, row0, w):
    t = x.shape[0]
    has_ffn = ffn is not None
    tile = lambda i: (i, 0)
    lay3 = lambda i: (layer, 0, 0)
    mod_spec = lambda l: pl.BlockSpec((1, 6, MOD_ROWS, D_MODEL), lambda i: (l, 0, 0, 0))
    in_specs = [pl.BlockSpec((TM, D_MODEL), tile)]
    args = [x]
    if has_ffn:
        in_specs += [pl.BlockSpec((TM, D_MODEL), tile), mod_spec(layer - 1)]
        args += [ffn, mod]
    in_specs += [
        mod_spec(layer),
        pl.BlockSpec((1, 1, D_MODEL), lay3),
        pl.BlockSpec((1, D_MODEL, MAIN_COLS), lay3),
        pl.BlockSpec((1, D_MODEL, LANES), lay3),
        pl.BlockSpec((1, 3, CONV_W), lay3),
        pl.BlockSpec((1, 3, 2 * QK_W + V_W), lay3),
        pl.BlockSpec((1, 8, LANES), lay3),
    ]
    args += [mod, w["norm1_w"], w["w_main"], w["w_small"], w["conv_a_w"], w["qkv_conv_w"],
             w["gate_params"]]
    out_specs, out_shape = [], []
    if has_ffn:
        out_specs.append(pl.BlockSpec((TM, D_MODEL), tile))
        out_shape.append(jax.ShapeDtypeStruct((t, D_MODEL), F32))
    out_specs.append(pl.BlockSpec((TM, CONV_W), tile))
    out_shape.append(jax.ShapeDtypeStruct((t, CONV_W), BF16))
    for _ in range(4):
        out_specs.append(pl.BlockSpec((TM, QK_W), tile))
        out_shape.append(jax.ShapeDtypeStruct((t, QK_W), F32))
    out_specs.append(pl.BlockSpec((TM, LANES), tile))
    out_shape.append(jax.ShapeDtypeStruct((t, LANES), F32))
    outs = pl.pallas_call(
        functools.partial(_pre_kernel, row_len, tiles_per_batch, row0, has_ffn),
        grid=(t // TM,),
        in_specs=in_specs,
        out_specs=out_specs,
        out_shape=out_shape,
        compiler_params=_params("arbitrary"),
        name=f"pre_mixer_l{layer}_r{row_len}",
    )(*args)
    if not has_ffn:
        outs = [x] + list(outs)
    return outs


def _gdn_kernel(has_s0, want_state, nblk, *refs):
    refs = list(refs)
    fwd_in = refs[0:4]
    bwd_in = refs[4:8]
    pos = 8
    s0_ref = None
    if has_s0:
        s0_ref = refs[pos]
        pos += 1
    of_ref, ob_ref = refs[pos], refs[pos + 1]
    pos += 2
    so_ref = None
    if want_state:
        so_ref = refs[pos]
        pos += 1
    s_scr, gf_scr, gb_scr = refs[pos:pos + 3]

    j = pl.program_id(1)

    @pl.when(j == 0)
    def _():
        if has_s0:
            s_scr[...] = s0_ref[0]
        else:
            s_scr[...] = jnp.zeros_like(s_scr)

    pc = lax.broadcasted_iota(jnp.int32, (TM, 1), 0) & (CHUNK - 1)
    g = fwd_in[3][0]
    for s in (1, 2, 4, 8, 16, 32):
        g = g + jnp.where(pc >= s, pltpu.roll(g, s, 0), 0.0)
    gf_scr[...] = g
    g = bwd_in[3][0]
    for s in (1, 2, 4, 8, 16, 32):
        g = g + jnp.where(pc < CHUNK - s, pltpu.roll(g, TM - s, 0), 0.0)
    gb_scr[...] = g

    ii = lax.broadcasted_iota(jnp.int32, (2 * CHUNK, 2 * CHUNK), 0)
    jj = lax.broadcasted_iota(jnp.int32, (2 * CHUNK, 2 * CHUNK), 1)
    same = (ii // CHUNK) == (jj // CHUNK)
    eye = (ii == jj).astype(F32)
    masks = (
        ((same & (ii >= jj)).astype(F32), (same & (ii > jj)).astype(F32)),
        ((same & (ii <= jj)).astype(F32), (same & (ii < jj)).astype(F32)),
    )

    def colb(arr, lane):
        return jnp.broadcast_to(arr[:, lane:lane + 1], (CHUNK, LANES))

    def stack2(a, b):
        return jnp.concatenate([a, b], axis=0)

    def chunk_step(ci, carry):
        for d in (0, 1):
            q_ref, k_ref, v_ref, sm_ref = fwd_in if d == 0 else bwd_in
            g_scr = gf_scr if d == 0 else gb_scr
            o_ref = of_ref if d == 0 else ob_ref
            r0 = pl.multiple_of((ci if d == 0 else (TM // CHUNK - 1 - ci)) * CHUNK, CHUNK)
            rows = pl.ds(r0, CHUNK)
            m_incl, m_strict = masks[d]
            gc = g_scr[rows, :]
            sm = sm_ref[0, rows, :]
            end_row = CHUNK - 1 if d == 0 else 0
            for p in range(DN_HEADS // 2):
                h0, h1 = 2 * p, 2 * p + 1
                hs = lambda hd: slice(hd * DK, (hd + 1) * DK)
                g2 = stack2(colb(gc, 2 * DN_HEADS + DN_HEADS * d + h0),
                            colb(gc, 2 * DN_HEADS + DN_HEADS * d + h1))
                b2 = stack2(colb(sm, DN_HEADS * d + h0), colb(sm, DN_HEADS * d + h1))
                q2 = stack2(q_ref[0, rows, hs(h0)], q_ref[0, rows, hs(h1)])
                k2 = stack2(k_ref[0, rows, hs(h0)], k_ref[0, rows, hs(h1)])
                v2 = stack2(v_ref[0, rows, hs(h0)], v_ref[0, rows, hs(h1)])
                k2b = k2.astype(BF16)

                diff = g2 - g2.T
                e = jnp.exp(diff * m_incl)
                eg = jnp.exp(g2)
                a = _dot_nt(k2b, k2b) * b2 * (e * m_strict)
                t_inv = eye - a
                pw = a
                for _ in range(5):
                    pw = _dot_split(pw, pw)
                    t_inv = t_inv + _dot_split(t_inv, pw)
                rhs = jnp.concatenate([v2 * b2, k2 * (b2 * eg)], axis=1).astype(BF16)
                uw = _dot(t_inv.astype(BF16), rhs)
                qk = (_dot_nt(q2.astype(BF16), k2b) * (e * m_incl)).astype(BF16)
                qd = q2 * eg
                g_end = stack2(jnp.broadcast_to(g2[end_row:end_row + 1, :], (CHUNK, LANES)),
                               jnp.broadcast_to(g2[CHUNK + end_row:CHUNK + end_row + 1, :],
                                                (CHUNK, LANES)))
                k_end = (k2 * jnp.exp(g_end - g2)).astype(BF16)

                v_new, q_s = [], []
                for hh, hd in enumerate((h0, h1)):
                    sl = slice(hh * CHUNK, (hh + 1) * CHUNK)
                    lhs = stack2(uw[sl, DV:], qd[sl]).astype(BF16)
                    ws = _dot(lhs, s_scr[d, hd].astype(BF16))
                    v_new.append(uw[sl, :DV] - ws[:CHUNK])
                    q_s.append(ws[CHUNK:])
                vn2 = stack2(*v_new).astype(BF16)
                o2 = stack2(*q_s) + _dot(qk, vn2)
                for hh, hd in enumerate((h0, h1)):
                    sl = slice(hh * CHUNK, (hh + 1) * CHUNK)
                    o_ref[0, rows, hs(hd)] = o2[sl]
                    decay = jnp.exp(g2[hh * CHUNK + end_row: hh * CHUNK + end_row + 1, :])
                    s_scr[d, hd] = s_scr[d, hd] * decay + _dot_tn(k_end[sl], vn2[sl])
        return carry

    lax.fori_loop(0, TM // CHUNK, chunk_step, 0)

    if want_state:
        @pl.when(j == nblk - 1)
        def _():
            so_ref[0] = s_scr[...]


def _gdn(q, k, v, small, s0, want_state, name):
    b, l, _ = q.shape
    nblk = l // TM
    fwd = lambda bi, j: (bi, j, 0)
    bwd = lambda bi, j: (bi, nblk - 1 - j, 0)
    st = lambda bi, j: (bi, 0, 0, 0, 0)
    in_specs, args = [], []
    for imap in (fwd, bwd):
        in_specs += [pl.BlockSpec((1, TM, QK_W), imap)] * 3 + [pl.BlockSpec((1, TM, LANES), imap)]
        args += [q, k, v, small]
    if s0 is not None:
        in_specs.append(pl.BlockSpec((1, 2, DN_HEADS, DK, DV), st))
        args.append(s0)
    out_specs = [pl.BlockSpec((1, TM, V_W), fwd), pl.BlockSpec((1, TM, V_W), bwd)]
    out_shape = [jax.ShapeDtypeStruct((b, l, V_W), F32)] * 2
    if want_state:
        out_specs.append(pl.BlockSpec((1, 2, DN_HEADS, DK, DV), st))
        out_shape.append(jax.ShapeDtypeStruct((b, 2, DN_HEADS, DK, DV), F32))
    return pl.pallas_call(
        functools.partial(_gdn_kernel, s0 is not None, want_state, nblk),
        grid=(b, nblk),
        in_specs=in_specs,
        out_specs=out_specs,
        out_shape=out_shape,
        scratch_shapes=[
            pltpu.VMEM((2, DN_HEADS, DK, DV), F32),
            pltpu.VMEM((TM, LANES), F32),
            pltpu.VMEM((TM, LANES), F32),
        ],
        compiler_params=_params("arbitrary", "arbitrary"),
        name=name,
    )(*args)


def _post_kernel(tiles_per_batch, row0, x_ref, of_ref, ob_ref, g_ref, yc_ref, mod_ref, onw_ref,
                 wo_ref, n2_ref, xo_ref, h2_ref):
    row = row0 + pl.program_id(0) // tiles_per_batch if tiles_per_batch else row0
    o = of_ref[...] + ob_ref[...]
    gated = []
    for hd in range(DN_HEADS):
        oh = o[:, hd * DV:(hd + 1) * DV]
        on = oh * lax.rsqrt(_mean_sq(oh) + EPS) * onw_ref[0]
        gated.append((on * g_ref[:, hd * DV:(hd + 1) * DV]).astype(BF16))
    og = jnp.concatenate(gated, axis=1)
    y = _dot(yc_ref[...], wo_ref[0, 0:CONV_W, :]) + _dot(og, wo_ref[0, CONV_W:, :])
    x = x_ref[...] + _mod_row(mod_ref, 2, row) * y
    xo_ref[...] = x
    xn = x * lax.rsqrt(_mean_sq(x) + EPS)
    h2 = xn * (n2_ref[0] * (1.0 + _mod_row(mod_ref, 4, row))) + _mod_row(mod_ref, 3, row)
    h2_ref[...] = h2.astype(BF16)


def _post_mixer(layer, x, o_f, o_b, gate, yconv, mod, tiles_per_batch, row0, w, tag):
    t = x.shape[0]
    tile = lambda i: (i, 0)
    lay3 = lambda i: (layer, 0, 0)
    return pl.pallas_call(
        functools.partial(_post_kernel, tiles_per_batch, row0),
        grid=(t // TM,),
        in_specs=[
            pl.BlockSpec((TM, D_MODEL), tile),
            pl.BlockSpec((TM, V_W), tile),
            pl.BlockSpec((TM, V_W), tile),
            pl.BlockSpec((TM, V_W), tile),
            pl.BlockSpec((TM, CONV_W), tile),
            pl.BlockSpec((1, 6, MOD_ROWS, D_MODEL), lambda i: (layer, 0, 0, 0)),
            pl.BlockSpec((1, 1, DV), lay3),
            pl.BlockSpec((1, D_MODEL, D_MODEL), lay3),
            pl.BlockSpec((1, 1, D_MODEL), lay3),
        ],
        out_specs=[pl.BlockSpec((TM, D_MODEL), tile), pl.BlockSpec((TM, D_MODEL), tile)],
        out_shape=[jax.ShapeDtypeStruct((t, D_MODEL), F32), jax.ShapeDtypeStruct((t, D_MODEL), BF16)],
        compiler_params=_params("arbitrary"),
        name=f"post_mixer_l{layer}_{tag}",
    )(x, o_f, o_b, gate, yconv, mod, w["o_norm_w"], w["w_out"], w["norm2_w"])


def _ffn_kernel(row_len, multi_row, h_ref, wu_ref, wg_ref, cw_ref, wd_ref, o_ref):
    f = pl.program_id(1)
    hb = h_ref[...]
    u = _dot(hb, wu_ref[0])
    gt = _dot(hb, wg_ref[0])
    col = lax.broadcasted_iota(jnp.int32, (TB, 1), 0) & (row_len - 1)
    gm = jnp.where(col == 0, 0.0, pltpu.roll(gt, 1, 0))
    gp = jnp.where(col == row_len - 1, 0.0, pltpu.roll(gt, TB - 1, 0))

    def hrow(ky):
        return (cw_ref[0, 3 * ky:3 * ky + 1, :] * gm + cw_ref[0, 3 * ky + 1:3 * ky + 2, :] * gt
                + cw_ref[0, 3 * ky + 2:3 * ky + 3, :] * gp)

    conv = hrow(1)
    if multi_row:
        z = jnp.zeros((row_len, FB), F32)
        conv = conv + jnp.concatenate([z, hrow(0)[:TB - row_len]], axis=0)
        conv = conv + jnp.concatenate([hrow(2)[row_len:], z], axis=0)
    act = (jax.nn.gelu(conv) * u).astype(BF16)
    contrib = _dot(act, wd_ref[0])

    @pl.when(f == 0)
    def _():
        o_ref[...] = contrib

    @pl.when(f != 0)
    def _():
        o_ref[...] += contrib


def _conv_glu(layer, h2, row_len, multi_row, w, tag):
    t = h2.shape[0]
    nf = D_FF // FB
    return pl.pallas_call(
        functools.partial(_ffn_kernel, row_len, multi_row),
        grid=(t // TB, nf),
        in_specs=[
            pl.BlockSpec((TB, D_MODEL), lambda i, f: (i, 0)),
            pl.BlockSpec((1, D_MODEL, FB), lambda i, f: (layer, 0, f)),
            pl.BlockSpec((1, D_MODEL, FB), lambda i, f: (layer, 0, nf + f)),
            pl.BlockSpec((1, 9, FB), lambda i, f: (layer, 0, f)),
            pl.BlockSpec((1, FB, D_MODEL), lambda i, f: (layer, f, 0)),
        ],
        out_specs=pl.BlockSpec((TB, D_MODEL), lambda i, f: (i, 0)),
        out_shape=jax.ShapeDtypeStruct((t, D_MODEL), F32),
        compiler_params=_params("arbitrary", "arbitrary"),
        name=f"conv_glu_l{layer}_{tag}",
    )(h2, w["w_up"], w["w_up"], w["ffn_conv_w"], w["w_down"])


def _final_kernel(tiles_per_batch, row0, x_ref, ffn_ref, mod_ref, w_ref, o_ref):
    row = row0 + pl.program_id(0) // tiles_per_batch if tiles_per_batch else row0
    x = x_ref[...] + _mod_row(mod_ref, 5, row) * ffn_ref[...]
    o_ref[...] = x * lax.rsqrt(_mean_sq(x) + EPS) * w_ref[...]


def _final_norm(x, ffn, mod, tiles_per_batch, row0, w, tag):
    t = x.shape[0]
    tile = lambda i: (i, 0)
    return pl.pallas_call(
        functools.partial(_final_kernel, tiles_per_batch, row0),
        grid=(t // TM,),
        in_specs=[
            pl.BlockSpec((TM, D_MODEL), tile),
            pl.BlockSpec((TM, D_MODEL), tile),
            pl.BlockSpec((1, 6, MOD_ROWS, D_MODEL), lambda i: (DEPTH - 1, 0, 0, 0)),
            pl.BlockSpec((1, D_MODEL), lambda i: (0, 0)),
        ],
        out_specs=pl.BlockSpec((TM, D_MODEL), tile),
        out_shape=jax.ShapeDtypeStruct((t, D_MODEL), F32),
        compiler_params=_params("arbitrary"),
        name=f"final_norm_{tag}",
    )(x, ffn, mod, w)


def kernel(x_prompt, x_sample, state_deltanet, c, c_ctx, w_ada, b_ada, norm1_w, w_in, conv_a_w,
           qkv_conv_w, a_log, dt_bias, o_norm_w, w_out, norm2_w, w_up, ffn_conv_w, w_down,
           final_norm_w):
    b_ctx, ctx_len, _ = x_prompt.shape
    b_lat, lat_len, _ = x_sample.shape
    assert ctx_len == TM and lat_len == TB and lat_len % GRID_W == 0 and b_lat + 1 <= MOD_ROWS
    assert (b_ctx * ctx_len) % TB == 0

    c_all = jnp.concatenate(
        [c_ctx[None, :], c, jnp.zeros((MOD_ROWS - 1 - b_lat, D_MODEL), F32)], axis=0)
    mod = _modulation(c_all, w_ada, b_ada)

    gate_rows = jnp.stack([a_log.reshape(DEPTH, -1), dt_bias.reshape(DEPTH, -1)], axis=1)
    gate_params = jnp.pad(gate_rows, ((0, 0), (0, 6), (2 * DN_HEADS, LANES - N_GATE_COLS)))
    w = {
        "norm1_w": norm1_w.reshape(DEPTH, 1, D_MODEL),
        "w_main": w_in[:, :, :MAIN_COLS].astype(BF16),
        "w_small": jnp.pad(w_in[:, :, MAIN_COLS:], ((0, 0), (0, 0), (0, LANES - N_GATE_COLS))).astype(BF16),
        "conv_a_w": conv_a_w,
        "qkv_conv_w": qkv_conv_w,
        "gate_params": gate_params,
        "o_norm_w": o_norm_w.reshape(DEPTH, 1, DV),
        "w_out": w_out.astype(BF16),
        "norm2_w": norm2_w.reshape(DEPTH, 1, D_MODEL),
        "w_up": w_up.astype(BF16),
        "ffn_conv_w": ffn_conv_w.reshape(DEPTH, 9, D_FF),
        "w_down": w_down.astype(BF16),
    }

    streams = {
        "ctx": dict(x=x_prompt.reshape(b_ctx * ctx_len, D_MODEL), b=b_ctx, l=ctx_len, row_len=ctx_len,
                    tpb=0, row0=0),
        "lat": dict(x=x_sample.reshape(b_lat * lat_len, D_MODEL), b=b_lat, l=lat_len, row_len=GRID_W,
                    tpb=lat_len // TM, row0=1),
    }
    ctx_states = []
    for tag, s in streams.items():
        s["ffn"] = None
    for layer in range(DEPTH):
        for tag, s in streams.items():
            x, yconv, q, k, v, gate, small = _pre_mixer(layer, s["x"], s["ffn"], mod, s["row_len"],
                                                        s["tpb"], s["row0"], w)
            shp = lambda a: a.reshape(s["b"], s["l"], a.shape[-1])
            if tag == "ctx":
                o_f, o_b, s_fin = _gdn(shp(q), shp(k), shp(v), shp(small), None, True,
                                       f"gdn_l{layer}_ctx")
                ctx_states.append(s_fin)
            else:
                o_f, o_b = _gdn(shp(q), shp(k), shp(v), shp(small), state_deltanet[:, layer], False,
                                f"gdn_l{layer}_lat")
            flat = lambda a: a.reshape(s["b"] * s["l"], a.shape[-1])
            x_mid, h2 = _post_mixer(layer, x, flat(o_f), flat(o_b), gate, yconv, mod, s["tpb"],
                                    s["row0"], w, tag)
            s["x"] = x_mid
            s["ffn"] = _conv_glu(layer, h2, s["row_len"], s["l"] // s["row_len"] > 1, w, tag)
    outs = {}
    for tag, s in streams.items():
        y = _final_norm(s["x"], s["ffn"], mod, s["tpb"], s["row0"], final_norm_w.reshape(1, D_MODEL), tag)
        outs[tag] = y.reshape(s["b"], s["l"], D_MODEL)
    new_state = jnp.stack(ctx_states, axis=1)
    return (outs["ctx"], outs["lat"], new_state)
```

```python
import functools

import jax
import jax.numpy as jnp
from jax import lax
from jax.experimental import pallas as pl
from jax.experimental.pallas import tpu as pltpu

D_MODEL = 1024
DEPTH = 2
GRID_W = 64
CONV_W = D_MODEL // 2
DN_HEADS = 4
DK = 128
DV = 128
QK_W = DN_HEADS * DK
V_W = DN_HEADS * DV
D_FF = 2816
CHUNK = 64
EPS = 1e-6
MAIN_COLS = 3 * CONV_W + 2 * QK_W + 2 * V_W
N_GATE_COLS = 4 * DN_HEADS

LANES = 128
TM = 256
TB = 2048
FB = 256
INV_BASE = 16
MOD_ROWS = 16
VMEM_LIMIT = 56 * 1024 * 1024

BF16 = jnp.bfloat16
F32 = jnp.float32


def _dot(a, b):
    return jnp.dot(a, b, preferred_element_type=F32)


def _dot_nt(a, b):
    return lax.dot_general(a, b, (((1,), (1,)), ((), ())), preferred_element_type=F32)


def _dot_tn(a, b):
    return lax.dot_general(a, b, (((0,), (0,)), ((), ())), preferred_element_type=F32)


def _dot_split(a, b):
    ah = a.astype(BF16)
    al = (a - ah.astype(F32)).astype(BF16)
    bh = b.astype(BF16)
    bl = (b - bh.astype(F32)).astype(BF16)
    return _dot(ah, bh) + (_dot(ah, bl) + _dot(al, bh))


def _silu(x):
    return x * jax.nn.sigmoid(x)


def _mean_sq(x):
    return jnp.mean(x * x, axis=-1, keepdims=True)


def _params(*sem):
    return pltpu.CompilerParams(dimension_semantics=sem, vmem_limit_bytes=VMEM_LIMIT)


def _mod_kernel(c_ref, w_ref, b_ref, o_ref):
    a = _silu(c_ref[...]).astype(BF16)
    o_ref[0, 0] = _dot(a, w_ref[0].astype(BF16)) + b_ref[0]


def _modulation(c_all, w_ada, b_ada):
    return pl.pallas_call(
        _mod_kernel,
        grid=(DEPTH, 6),
        in_specs=[
            pl.BlockSpec((MOD_ROWS, D_MODEL), lambda l, n: (0, 0)),
            pl.BlockSpec((1, D_MODEL, D_MODEL), lambda l, n: (l, 0, n)),
            pl.BlockSpec((1, 1, D_MODEL), lambda l, n: (l, 0, n)),
        ],
        out_specs=pl.BlockSpec((1, 1, MOD_ROWS, D_MODEL), lambda l, n: (l, n, 0, 0)),
        out_shape=jax.ShapeDtypeStruct((DEPTH, 6, MOD_ROWS, D_MODEL), F32),
        compiler_params=_params("arbitrary", "arbitrary"),
        name="modulation",
    )(c_all, w_ada, b_ada.reshape(DEPTH, 1, 6 * D_MODEL))


def _mod_row(mod_ref, which, row):
    return mod_ref[0, which, pl.ds(row, 1), :]


def _conv3(u, w_ref, lo, first, last):
    n = u.shape[0]
    width = u.shape[1]
    up = jnp.where(first, 0.0, pltpu.roll(u, 1, 0))
    un = jnp.where(last, 0.0, pltpu.roll(u, n - 1, 0))
    w0 = w_ref[0, 0:1, lo:lo + width]
    w1 = w_ref[0, 1:2, lo:lo + width]
    w2 = w_ref[0, 2:3, lo:lo + width]
    return w0 * up + w1 * u + w2 * un


def _pre_kernel(row_len, tiles_per_batch, row0, has_ffn, *refs):
    if has_ffn:
        (x_ref, ffn_ref, modp_ref, mod_ref, n1_ref, wm_ref, ws_ref, cw_ref, qw_ref, gp_ref,
         xo_ref, yc_ref, q_ref, k_ref, v_ref, g_ref, sm_ref) = refs
    else:
        (x_ref, mod_ref, n1_ref, wm_ref, ws_ref, cw_ref, qw_ref, gp_ref,
         yc_ref, q_ref, k_ref, v_ref, g_ref, sm_ref) = refs
    row = row0 + pl.program_id(0) // tiles_per_batch if tiles_per_batch else row0
    x = x_ref[...]
    if has_ffn:
        x = x + _mod_row(modp_ref, 5, row) * ffn_ref[...]
        xo_ref[...] = x
    xn = x * lax.rsqrt(_mean_sq(x) + EPS)
    h = xn * (n1_ref[0] * (1.0 + _mod_row(mod_ref, 1, row))) + _mod_row(mod_ref, 0, row)
    hb = h.astype(BF16)

    pos = lax.broadcasted_iota(jnp.int32, (TM, 1), 0) & (row_len - 1)
    first = pos == 0
    last = pos == row_len - 1

    cb = _dot(hb, wm_ref[0, :, 0:CONV_W])
    cc = _dot(hb, wm_ref[0, :, CONV_W:2 * CONV_W])
    cx = _dot(hb, wm_ref[0, :, 2 * CONV_W:3 * CONV_W])
    yc_ref[...] = (cb * _conv3(cc * cx, cw_ref, 0, first, last)).astype(BF16)

    base = 3 * CONV_W
    for idx, out_ref in enumerate((q_ref, k_ref, v_ref)):
        z = _dot(hb, wm_ref[0, :, base + idx * QK_W: base + (idx + 1) * QK_W])
        a = _silu(_conv3(z, qw_ref, idx * QK_W, first, last))
        if idx < 2:
            scale = DK ** -0.5 if idx == 0 else 1.0
            for hd in range(DN_HEADS):
                ah = a[:, hd * DK:(hd + 1) * DK]
                ss = jnp.sum(ah * ah, axis=-1, keepdims=True)
                out_ref[:, hd * DK:(hd + 1) * DK] = ah * (lax.rsqrt(ss + EPS) * scale)
        else:
            out_ref[...] = a
    g_ref[...] = _silu(_dot(hb, wm_ref[0, :, base + 3 * QK_W: base + 3 * QK_W + V_W]))

    zs = _dot(hb, ws_ref[0])
    lane = lax.broadcasted_iota(jnp.int32, (TM, LANES), 1)
    a_log = gp_ref[0, 0:1, :]
    dt_bias = gp_ref[0, 1:2, :]
    log_alpha = -jnp.exp(a_log) * jax.nn.softplus(zs + dt_bias)
    sm_ref[...] = jnp.where(lane < 2 * DN_HEADS, jax.nn.sigmoid(zs),
                            jnp.where(lane < N_GATE_COLS, log_alpha, 0.0))


def _pre_mixer(layer, x, ffn, mod, row_len, tiles_per_batch, row0, w):
    t = x.shape[0]
    has_ffn = ffn is not None
    tile = lambda i: (i, 0)
    lay3 = lambda i: (layer, 0, 0)
    mod_spec = lambda l: pl.BlockSpec((1, 6, MOD_ROWS, D_MODEL), lambda i: (l, 0, 0, 0))
    in_specs = [pl.BlockSpec((TM, D_MODEL), tile)]
    args = [x]
    if has_ffn:
        in_specs += [pl.BlockSpec((TM, D_MODEL), tile), mod_spec(layer - 1)]
        args += [ffn, mod]
    in_specs += [
        mod_spec(layer),
        pl.BlockSpec((1, 1, D_MODEL), lay3),
        pl.BlockSpec((1, D_MODEL, MAIN_COLS), lay3),
        pl.BlockSpec((1, D_MODEL, LANES), lay3),
        pl.BlockSpec((1, 3, CONV_W), lay3),
        pl.BlockSpec((1, 3, 2 * QK_W + V_W), lay3),
        pl.BlockSpec((1, 8, LANES), lay3),
    ]
    args += [mod, w["norm1_w"], w["w_main"], w["w_small"], w["conv_a_w"], w["qkv_conv_w"],
             w["gate_params"]]
    out_specs, out_shape = [], []
    if has_ffn:
        out_specs.append(pl.BlockSpec((TM, D_MODEL), tile))
        out_shape.append(jax.ShapeDtypeStruct((t, D_MODEL), F32))
    out_specs.append(pl.BlockSpec((TM, CONV_W), tile))
    out_shape.append(jax.ShapeDtypeStruct((t, CONV_W), BF16))
    for _ in range(4):
        out_specs.append(pl.BlockSpec((TM, QK_W), tile))
        out_shape.append(jax.ShapeDtypeStruct((t, QK_W), F32))
    out_specs.append(pl.BlockSpec((TM, LANES), tile))
    out_shape.append(jax.ShapeDtypeStruct((t, LANES), F32))
    outs = pl.pallas_call(
        functools.partial(_pre_kernel, row_len, tiles_per_batch, row0, has_ffn),
        grid=(t // TM,),
        in_specs=in_specs,
        out_specs=out_specs,
        out_shape=out_shape,
        compiler_params=_params("arbitrary"),
        name=f"pre_mixer_l{layer}_r{row_len}",
    )(*args)
    if not has_ffn:
        outs = [x] + list(outs)
    return outs


def _gdn_kernel(has_s0, want_state, nblk, *refs):
    refs = list(refs)
    fwd_in = refs[0:4]
    bwd_in = refs[4:8]
    pos = 8
    s0_ref = None
    if has_s0:
        s0_ref = refs[pos]
        pos += 1
    of_ref, ob_ref = refs[pos], refs[pos + 1]
    pos += 2
    so_ref = None
    if want_state:
        so_ref = refs[pos]
        pos += 1
    s_scr, gf_scr, gb_scr = refs[pos:pos + 3]

    j = pl.program_id(1)

    @pl.when(j == 0)
    def _():
        if has_s0:
            s_scr[...] = s0_ref[0]
        else:
            s_scr[...] = jnp.zeros_like(s_scr)

    pc = lax.broadcasted_iota(jnp.int32, (TM, 1), 0) & (CHUNK - 1)
    g = fwd_in[3][0]
    for s in (1, 2, 4, 8, 16, 32):
        g = g + jnp.where(pc >= s, pltpu.roll(g, s, 0), 0.0)
    gf_scr[...] = g
    g = bwd_in[3][0]
    for s in (1, 2, 4, 8, 16, 32):
        g = g + jnp.where(pc < CHUNK - s, pltpu.roll(g, TM - s, 0), 0.0)
    gb_scr[...] = g

    ii = lax.broadcasted_iota(jnp.int32, (2 * CHUNK, 2 * CHUNK), 0)
    jj = lax.broadcasted_iota(jnp.int32, (2 * CHUNK, 2 * CHUNK), 1)
    blk = lambda n: (ii // n) == (jj // n)
    same = blk(CHUNK)
    eye = (ii == jj).astype(F32)
    masks = (
        ((same & (ii >= jj)).astype(F32), (same & (ii > jj)).astype(F32)),
        ((same & (ii <= jj)).astype(F32), (same & (ii < jj)).astype(F32)),
    )
    blk_diag = blk(INV_BASE).astype(F32)
    merge_masks = []
    n = INV_BASE
    while n < CHUNK:
        merge_masks.append((blk(2 * n) & jnp.logical_not(blk(n))).astype(F32))
        n *= 2

    def colb(arr, lane):
        return jnp.broadcast_to(arr[:, lane:lane + 1], (CHUNK, LANES))

    def stack2(a, b):
        return jnp.concatenate([a, b], axis=0)

    def mm(a, b):
        return _dot(a.astype(BF16), b.astype(BF16))

    n_chunks = TM // CHUNK
    n_pairs = DN_HEADS // 2
    hs = lambda hd: slice(hd * DK, (hd + 1) * DK)
    chains = [(d, c, p) for d in (0, 1) for c in range(n_chunks) for p in range(n_pairs)]
    st = {}
    for key in chains:
        d, c, p = key
        q_ref, k_ref, v_ref, sm_ref = fwd_in if d == 0 else bwd_in
        g_scr = gf_scr if d == 0 else gb_scr
        rows = slice(c * CHUNK, (c + 1) * CHUNK)
        h0, h1 = 2 * p, 2 * p + 1
        gc = g_scr[rows, :]
        sm = sm_ref[0, rows, :]
        g2 = stack2(colb(gc, 2 * DN_HEADS + DN_HEADS * d + h0),
                    colb(gc, 2 * DN_HEADS + DN_HEADS * d + h1))
        b2 = stack2(colb(sm, DN_HEADS * d + h0), colb(sm, DN_HEADS * d + h1))
        q2 = stack2(q_ref[0, rows, hs(h0)], q_ref[0, rows, hs(h1)])
        k2 = stack2(k_ref[0, rows, hs(h0)], k_ref[0, rows, hs(h1)])
        v2 = stack2(v_ref[0, rows, hs(h0)], v_ref[0, rows, hs(h1)])
        m_incl, m_strict = masks[d]
        e = jnp.exp((g2 - g2.T) * m_incl)
        eg = jnp.exp(g2)
        end_row = CHUNK - 1 if d == 0 else 0
        g_end = stack2(jnp.broadcast_to(g2[end_row:end_row + 1, :], (CHUNK, LANES)),
                       jnp.broadcast_to(g2[CHUNK + end_row:CHUNK + end_row + 1, :], (CHUNK, LANES)))
        st[key] = dict(
            k2b=k2.astype(BF16), q2b=q2.astype(BF16), b2=b2,
            dec_strict=e * m_strict, dec_incl=e * m_incl,
            rhs=jnp.concatenate([v2 * b2, k2 * (b2 * eg)], axis=1).astype(BF16),
            qd=q2 * eg, k_end=(k2 * jnp.exp(g_end - g2)).astype(BF16),
            decay=[jnp.exp(g2[hh * CHUNK + end_row: hh * CHUNK + end_row + 1, :]) for hh in (0, 1)],
        )
    for key in chains:
        c_ = st[key]
        c_["a"] = _dot_nt(c_["k2b"], c_["k2b"]) * c_["b2"] * c_["dec_strict"]
        c_["qk"] = (_dot_nt(c_["q2b"], c_["k2b"]) * c_["dec_incl"]).astype(BF16)
    for key in chains:
        c_ = st[key]
        c_["pw"] = c_["a"] * blk_diag
        c_["t"] = eye - c_["pw"]
    lvl = 2
    while lvl < INV_BASE:
        for key in chains:
            st[key]["pw"] = mm(st[key]["pw"], st[key]["pw"])
        for key in chains:
            st[key]["t"] = st[key]["t"] + mm(st[key]["t"], st[key]["pw"])
        lvl *= 2
    for m_off in merge_masks:
        for key in chains:
            st[key]["pw"] = mm(st[key]["a"] * m_off, st[key]["t"])
        for key in chains:
            st[key]["t"] = st[key]["t"] - mm(st[key]["t"], st[key]["pw"])
    for key in chains:
        st[key]["uw"] = _dot(st[key]["t"].astype(BF16), st[key]["rhs"])

    state = {(d, hd): s_scr[d, hd] for d in (0, 1) for hd in range(DN_HEADS)}
    for step in range(n_chunks):
        keys = [(d, step if d == 0 else n_chunks - 1 - step, p) for d in (0, 1) for p in range(n_pairs)]
        ws = {}
        for key in keys:
            d, c, p = key
            c_ = st[key]
            for hh in (0, 1):
                sl = slice(hh * CHUNK, (hh + 1) * CHUNK)
                lhs = stack2(c_["uw"][sl, DV:], c_["qd"][sl]).astype(BF16)
                ws[key, hh] = _dot(lhs, state[d, 2 * p + hh].astype(BF16))
        vn = {}
        for key in keys:
            d, c, p = key
            c_ = st[key]
            vn[key] = stack2(*[c_["uw"][hh * CHUNK:(hh + 1) * CHUNK, :DV] - ws[key, hh][:CHUNK]
                               for hh in (0, 1)]).astype(BF16)
            o2 = stack2(ws[key, 0][CHUNK:], ws[key, 1][CHUNK:]) + _dot(c_["qk"], vn[key])
            o_ref = of_ref if d == 0 else ob_ref
            for hh in (0, 1):
                o_ref[0, c * CHUNK:(c + 1) * CHUNK, hs(2 * p + hh)] = o2[hh * CHUNK:(hh + 1) * CHUNK]
        for key in keys:
            d, c, p = key
            c_ = st[key]
            for hh in (0, 1):
                sl = slice(hh * CHUNK, (hh + 1) * CHUNK)
                state[d, 2 * p + hh] = (state[d, 2 * p + hh] * c_["decay"][hh]
                                        + _dot_tn(c_["k_end"][sl], vn[key][sl]))
    for (d, hd), val in state.items():
        s_scr[d, hd] = val

    if want_state:
        @pl.when(j == nblk - 1)
        def _():
            so_ref[0] = s_scr[...]


def _gdn(q, k, v, small, s0, want_state, name):
    b, l, _ = q.shape
    nblk = l // TM
    fwd = lambda bi, j: (bi, j, 0)
    bwd = lambda bi, j: (bi, nblk - 1 - j, 0)
    st = lambda bi, j: (bi, 0, 0, 0, 0)
    in_specs, args = [], []
    for imap in (fwd, bwd):
        in_specs += [pl.BlockSpec((1, TM, QK_W), imap)] * 3 + [pl.BlockSpec((1, TM, LANES), imap)]
        args += [q, k, v, small]
    if s0 is not None:
        in_specs.append(pl.BlockSpec((1, 2, DN_HEADS, DK, DV), st))
        args.append(s0)
    out_specs = [pl.BlockSpec((1, TM, V_W), fwd), pl.BlockSpec((1, TM, V_W), bwd)]
    out_shape = [jax.ShapeDtypeStruct((b, l, V_W), F32)] * 2
    if want_state:
        out_specs.append(pl.BlockSpec((1, 2, DN_HEADS, DK, DV), st))
        out_shape.append(jax.ShapeDtypeStruct((b, 2, DN_HEADS, DK, DV), F32))
    return pl.pallas_call(
        functools.partial(_gdn_kernel, s0 is not None, want_state, nblk),
        grid=(b, nblk),
        in_specs=in_specs,
        out_specs=out_specs,
        out_shape=out_shape,
        scratch_shapes=[
            pltpu.VMEM((2, DN_HEADS, DK, DV), F32),
            pltpu.VMEM((TM, LANES), F32),
            pltpu.VMEM((TM, LANES), F32),
        ],
        compiler_params=_params("arbitrary", "arbitrary"),
        name=name,
    )(*args)


def _post_kernel(tiles_per_batch, row0, x_ref, of_ref, ob_ref, g_ref, yc_ref, mod_ref, onw_ref,
                 wo_ref, n2_ref, xo_ref, h2_ref):
    row = row0 + pl.program_id(0) // tiles_per_batch if tiles_per_batch else row0
    o = of_ref[...] + ob_ref[...]
    gated = []
    for hd in range(DN_HEADS):
        oh = o[:, hd * DV:(hd + 1) * DV]
        on = oh * lax.rsqrt(_mean_sq(oh) + EPS) * onw_ref[0]
        gated.append((on * g_ref[:, hd * DV:(hd + 1) * DV]).astype(BF16))
    og = jnp.concatenate(gated, axis=1)
    y = _dot(yc_ref[...], wo_ref[0, 0:CONV_W, :]) + _dot(og, wo_ref[0, CONV_W:, :])
    x = x_ref[...] + _mod_row(mod_ref, 2, row) * y
    xo_ref[...] = x
    xn = x * lax.rsqrt(_mean_sq(x) + EPS)
    h2 = xn * (n2_ref[0] * (1.0 + _mod_row(mod_ref, 4, row))) + _mod_row(mod_ref, 3, row)
    h2_ref[...] = h2.astype(BF16)


def _post_mixer(layer, x, o_f, o_b, gate, yconv, mod, tiles_per_batch, row0, w, tag):
    t = x.shape[0]
    tile = lambda i: (i, 0)
    lay3 = lambda i: (layer, 0, 0)
    return pl.pallas_call(
        functools.partial(_post_kernel, tiles_per_batch, row0),
        grid=(t // TM,),
        in_specs=[
            pl.BlockSpec((TM, D_MODEL), tile),
            pl.BlockSpec((TM, V_W), tile),
            pl.BlockSpec((TM, V_W), tile),
            pl.BlockSpec((TM, V_W), tile),
            pl.BlockSpec((TM, CONV_W), tile),
            pl.BlockSpec((1, 6, MOD_ROWS, D_MODEL), lambda i: (layer, 0, 0, 0)),
            pl.BlockSpec((1, 1, DV), lay3),
            pl.BlockSpec((1, D_MODEL, D_MODEL), lay3),
            pl.BlockSpec((1, 1, D_MODEL), lay3),
        ],
        out_specs=[pl.BlockSpec((TM, D_MODEL), tile), pl.BlockSpec((TM, D_MODEL), tile)],
        out_shape=[jax.ShapeDtypeStruct((t, D_MODEL), F32), jax.ShapeDtypeStruct((t, D_MODEL), BF16)],
        compiler_params=_params("arbitrary"),
        name=f"post_mixer_l{layer}_{tag}",
    )(x, o_f, o_b, gate, yconv, mod, w["o_norm_w"], w["w_out"], w["norm2_w"])


def _ffn_kernel(row_len, multi_row, h_ref, wu_ref, wg_ref, cw_ref, wd_ref, o_ref):
    f = pl.program_id(1)
    hb = h_ref[...]
    u = _dot(hb, wu_ref[0])
    gt = _dot(hb, wg_ref[0])
    col = lax.broadcasted_iota(jnp.int32, (TB, 1), 0) & (row_len - 1)
    gm = jnp.where(col == 0, 0.0, pltpu.roll(gt, 1, 0))
    gp = jnp.where(col == row_len - 1, 0.0, pltpu.roll(gt, TB - 1, 0))

    def hrow(ky):
        return (cw_ref[0, 3 * ky:3 * ky + 1, :] * gm + cw_ref[0, 3 * ky + 1:3 * ky + 2, :] * gt
                + cw_ref[0, 3 * ky + 2:3 * ky + 3, :] * gp)

    conv = hrow(1)
    if multi_row:
        z = jnp.zeros((row_len, FB), F32)
        conv = conv + jnp.concatenate([z, hrow(0)[:TB - row_len]], axis=0)
        conv = conv + jnp.concatenate([hrow(2)[row_len:], z], axis=0)
    act = (jax.nn.gelu(conv) * u).astype(BF16)
    contrib = _dot(act, wd_ref[0])

    @pl.when(f == 0)
    def _():
        o_ref[...] = contrib

    @pl.when(f != 0)
    def _():
        o_ref[...] += contrib


def _conv_glu(layer, h2, row_len, multi_row, w, tag):
    t = h2.shape[0]
    nf = D_FF // FB
    return pl.pallas_call(
        functools.partial(_ffn_kernel, row_len, multi_row),
        grid=(t // TB, nf),
        in_specs=[
            pl.BlockSpec((TB, D_MODEL), lambda i, f: (i, 0)),
            pl.BlockSpec((1, D_MODEL, FB), lambda i, f: (layer, 0, f)),
            pl.BlockSpec((1, D_MODEL, FB), lambda i, f: (layer, 0, nf + f)),
            pl.BlockSpec((1, 9, FB), lambda i, f: (layer, 0, f)),
            pl.BlockSpec((1, FB, D_MODEL), lambda i, f: (layer, f, 0)),
        ],
        out_specs=pl.BlockSpec((TB, D_MODEL), lambda i, f: (i, 0)),
        out_shape=jax.ShapeDtypeStruct((t, D_MODEL), F32),
        compiler_params=_params("arbitrary", "arbitrary"),
        name=f"conv_glu_l{layer}_{tag}",
    )(h2, w["w_up"], w["w_up"], w["ffn_conv_w"], w["w_down"])


def _final_kernel(tiles_per_batch, row0, x_ref, ffn_ref, mod_ref, w_ref, o_ref):
    row = row0 + pl.program_id(0) // tiles_per_batch if tiles_per_batch else row0
    x = x_ref[...] + _mod_row(mod_ref, 5, row) * ffn_ref[...]
    o_ref[...] = x * lax.rsqrt(_mean_sq(x) + EPS) * w_ref[...]


def _final_norm(x, ffn, mod, tiles_per_batch, row0, w, tag):
    t = x.shape[0]
    tile = lambda i: (i, 0)
    return pl.pallas_call(
        functools.partial(_final_kernel, tiles_per_batch, row0),
        grid=(t // TM,),
        in_specs=[
            pl.BlockSpec((TM, D_MODEL), tile),
            pl.BlockSpec((TM, D_MODEL), tile),
            pl.BlockSpec((1, 6, MOD_ROWS, D_MODEL), lambda i: (DEPTH - 1, 0, 0, 0)),
            pl.BlockSpec((1, D_MODEL), lambda i: (0, 0)),
        ],
        out_specs=pl.BlockSpec((TM, D_MODEL), tile),
        out_shape=jax.ShapeDtypeStruct((t, D_MODEL), F32),
        compiler_params=_params("arbitrary"),
        name=f"final_norm_{tag}",
    )(x, ffn, mod, w)


def kernel(x_prompt, x_sample, state_deltanet, c, c_ctx, w_ada, b_ada, norm1_w, w_in, conv_a_w,
           qkv_conv_w, a_log, dt_bias, o_norm_w, w_out, norm2_w, w_up, ffn_conv_w, w_down,
           final_norm_w):
    b_ctx, ctx_len, _ = x_prompt.shape
    b_lat, lat_len, _ = x_sample.shape
    assert ctx_len == TM and lat_len == TB and lat_len % GRID_W == 0 and b_lat + 1 <= MOD_ROWS
    assert (b_ctx * ctx_len) % TB == 0

    c_all = jnp.concatenate(
        [c_ctx[None, :], c, jnp.zeros((MOD_ROWS - 1 - b_lat, D_MODEL), F32)], axis=0)
    mod = _modulation(c_all, w_ada, b_ada)

    gate_rows = jnp.stack([a_log.reshape(DEPTH, -1), dt_bias.reshape(DEPTH, -1)], axis=1)
    gate_params = jnp.pad(gate_rows, ((0, 0), (0, 6), (2 * DN_HEADS, LANES - N_GATE_COLS)))
    w = {
        "norm1_w": norm1_w.reshape(DEPTH, 1, D_MODEL),
        "w_main": w_in[:, :, :MAIN_COLS].astype(BF16),
        "w_small": jnp.pad(w_in[:, :, MAIN_COLS:], ((0, 0), (0, 0), (0, LANES - N_GATE_COLS))).astype(BF16),
        "conv_a_w": conv_a_w,
        "qkv_conv_w": qkv_conv_w,
        "gate_params": gate_params,
        "o_norm_w": o_norm_w.reshape(DEPTH, 1, DV),
        "w_out": w_out.astype(BF16),
        "norm2_w": norm2_w.reshape(DEPTH, 1, D_MODEL),
        "w_up": w_up.astype(BF16),
        "ffn_conv_w": ffn_conv_w.reshape(DEPTH, 9, D_FF),
        "w_down": w_down.astype(BF16),
    }

    streams = {
        "ctx": dict(x=x_prompt.reshape(b_ctx * ctx_len, D_MODEL), b=b_ctx, l=ctx_len, row_len=ctx_len,
                    tpb=0, row0=0),
        "lat": dict(x=x_sample.reshape(b_lat * lat_len, D_MODEL), b=b_lat, l=lat_len, row_len=GRID_W,
                    tpb=lat_len // TM, row0=1),
    }
    ctx_states = []
    for tag, s in streams.items():
        s["ffn"] = None
    for layer in range(DEPTH):
        for tag, s in streams.items():
            x, yconv, q, k, v, gate, small = _pre_mixer(layer, s["x"], s["ffn"], mod, s["row_len"],
                                                        s["tpb"], s["row0"], w)
            shp = lambda a: a.reshape(s["b"], s["l"], a.shape[-1])
            if tag == "ctx":
                o_f, o_b, s_fin = _gdn(shp(q), shp(k), shp(v), shp(small), None, True,
                                       f"gdn_l{layer}_ctx")
                ctx_states.append(s_fin)
            else:
                o_f, o_b = _gdn(shp(q), shp(k), shp(v), shp(small), state_deltanet[:, layer], False,
                                f"gdn_l{layer}_lat")
            flat = lambda a: a.reshape(s["b"] * s["l"], a.shape[-1])
            x_mid, h2 = _post_mixer(layer, x, flat(o_f), flat(o_b), gate, yconv, mod, s["tpb"],
                                    s["row0"], w, tag)
            s["x"] = x_mid
            s["ffn"] = _conv_glu(layer, h2, s["row_len"], s["l"] // s["row_len"] > 1, w, tag)
    outs = {}
    for tag, s in streams.items():
        y = _final_norm(s["x"], s["ffn"], mod, s["tpb"], s["row0"], final_norm_w.reshape(1, D_MODEL), tag)
        outs[tag] = y.reshape(s["b"], s["l"], D_MODEL)
    new_state = jnp.stack(ctx_states, axis=1)
    return (outs["ctx"], outs["lat"], new_state)
```

```python
import functools

import jax
import jax.numpy as jnp
from jax import lax
from jax.experimental import pallas as pl
from jax.experimental.pallas import tpu as pltpu

D_MODEL = 1024
DEPTH = 2
GRID_W = 64
CONV_W = D_MODEL // 2
DN_HEADS = 4
DK = 128
DV = 128
QK_W = DN_HEADS * DK
V_W = DN_HEADS * DV
D_FF = 2816
CHUNK = 64
EPS = 1e-6
MAIN_COLS = 3 * CONV_W + 2 * QK_W + 2 * V_W
N_GATE_COLS = 4 * DN_HEADS

LANES = 128
TM = 256
TB = 2048
FB = 256
FFN_PARTS = 4
CONV_ROWS = 64
INV_BASE = 16
MOD_ROWS = 16
VMEM_LIMIT = 56 * 1024 * 1024

BF16 = jnp.bfloat16
F32 = jnp.float32


def _dot(a, b):
    return jnp.dot(a, b, preferred_element_type=F32)


def _dot_nt(a, b):
    return lax.dot_general(a, b, (((1,), (1,)), ((), ())), preferred_element_type=F32)


def _dot_tn(a, b):
    return lax.dot_general(a, b, (((0,), (0,)), ((), ())), preferred_element_type=F32)


def _dot_split(a, b):
    ah = a.astype(BF16)
    al = (a - ah.astype(F32)).astype(BF16)
    bh = b.astype(BF16)
    bl = (b - bh.astype(F32)).astype(BF16)
    return _dot(ah, bh) + (_dot(ah, bl) + _dot(al, bh))


def _silu(x):
    return x * jax.nn.sigmoid(x)


def _mean_sq(x):
    return jnp.mean(x * x, axis=-1, keepdims=True)


def _params(*sem):
    return pltpu.CompilerParams(dimension_semantics=sem, vmem_limit_bytes=VMEM_LIMIT)


def _mod_kernel(c_ref, w_ref, b_ref, o_ref):
    a = _silu(c_ref[...]).astype(BF16)
    o_ref[0, 0] = _dot(a, w_ref[0].astype(BF16)) + b_ref[0]


def _modulation(c_all, w_ada, b_ada):
    return pl.pallas_call(
        _mod_kernel,
        grid=(DEPTH, 6),
        in_specs=[
            pl.BlockSpec((MOD_ROWS, D_MODEL), lambda l, n: (0, 0)),
            pl.BlockSpec((1, D_MODEL, D_MODEL), lambda l, n: (l, 0, n)),
            pl.BlockSpec((1, 1, D_MODEL), lambda l, n: (l, 0, n)),
        ],
        out_specs=pl.BlockSpec((1, 1, MOD_ROWS, D_MODEL), lambda l, n: (l, n, 0, 0)),
        out_shape=jax.ShapeDtypeStruct((DEPTH, 6, MOD_ROWS, D_MODEL), F32),
        compiler_params=_params("arbitrary", "arbitrary"),
        name="modulation",
    )(c_all, w_ada, b_ada.reshape(DEPTH, 1, 6 * D_MODEL))


def _mod_row(mod_ref, which, row):
    return mod_ref[0, which, pl.ds(row, 1), :]


def _conv3(u, w_ref, lo, first, last):
    n = u.shape[0]
    width = u.shape[1]
    up = jnp.where(first, 0.0, pltpu.roll(u, 1, 0))
    un = jnp.where(last, 0.0, pltpu.roll(u, n - 1, 0))
    w0 = w_ref[0, 0:1, lo:lo + width]
    w1 = w_ref[0, 1:2, lo:lo + width]
    w2 = w_ref[0, 2:3, lo:lo + width]
    return w0 * up + w1 * u + w2 * un


def _pre_kernel(row_len, tiles_per_batch, row0, x_ref, mod_ref, n1_ref, wm_ref, ws_ref, cw_ref,
                qw_ref, gp_ref, yc_ref, q_ref, k_ref, v_ref, g_ref, sm_ref):
    row = row0 + pl.program_id(0) // tiles_per_batch if tiles_per_batch else row0
    x = x_ref[...]
    xn = x * lax.rsqrt(_mean_sq(x) + EPS)
    h = xn * (n1_ref[0] * (1.0 + _mod_row(mod_ref, 1, row))) + _mod_row(mod_ref, 0, row)
    hb = h.astype(BF16)

    pos = lax.broadcasted_iota(jnp.int32, (TM, 1), 0) & (row_len - 1)
    first = pos == 0
    last = pos == row_len - 1

    cb = _dot(hb, wm_ref[0, :, 0:CONV_W])
    cc = _dot(hb, wm_ref[0, :, CONV_W:2 * CONV_W])
    cx = _dot(hb, wm_ref[0, :, 2 * CONV_W:3 * CONV_W])
    yc_ref[...] = (cb * _conv3(cc * cx, cw_ref, 0, first, last)).astype(BF16)

    base = 3 * CONV_W
    for idx, out_ref in enumerate((q_ref, k_ref, v_ref)):
        z = _dot(hb, wm_ref[0, :, base + idx * QK_W: base + (idx + 1) * QK_W])
        a = _silu(_conv3(z, qw_ref, idx * QK_W, first, last))
        if idx < 2:
            scale = DK ** -0.5 if idx == 0 else 1.0
            for hd in range(DN_HEADS):
                ah = a[:, hd * DK:(hd + 1) * DK]
                ss = jnp.sum(ah * ah, axis=-1, keepdims=True)
                out_ref[:, hd * DK:(hd + 1) * DK] = ah * (lax.rsqrt(ss + EPS) * scale)
        else:
            out_ref[...] = a
    g_ref[...] = _silu(_dot(hb, wm_ref[0, :, base + 3 * QK_W: base + 3 * QK_W + V_W]))

    zs = _dot(hb, ws_ref[0])
    lane = lax.broadcasted_iota(jnp.int32, (TM, LANES), 1)
    a_log = gp_ref[0, 0:1, :]
    dt_bias = gp_ref[0, 1:2, :]
    log_alpha = -jnp.exp(a_log) * jax.nn.softplus(zs + dt_bias)
    sm_ref[...] = jnp.where(lane < 2 * DN_HEADS, jax.nn.sigmoid(zs),
                            jnp.where(lane < N_GATE_COLS, log_alpha, 0.0))


def _pre_mixer(layer, x, mod, row_len, tiles_per_batch, row0, w):
    t = x.shape[0]
    tile = lambda i: (i, 0)
    lay3 = lambda i: (layer, 0, 0)
    in_specs = [
        pl.BlockSpec((TM, D_MODEL), tile),
        pl.BlockSpec((1, 6, MOD_ROWS, D_MODEL), lambda i: (layer, 0, 0, 0)),
        pl.BlockSpec((1, 1, D_MODEL), lay3),
        pl.BlockSpec((1, D_MODEL, MAIN_COLS), lay3),
        pl.BlockSpec((1, D_MODEL, LANES), lay3),
        pl.BlockSpec((1, 3, CONV_W), lay3),
        pl.BlockSpec((1, 3, 2 * QK_W + V_W), lay3),
        pl.BlockSpec((1, 8, LANES), lay3),
    ]
    args = [x, mod, w["norm1_w"], w["w_main"], w["w_small"], w["conv_a_w"], w["qkv_conv_w"],
            w["gate_params"]]
    out_specs = [pl.BlockSpec((TM, CONV_W), tile)]
    out_shape = [jax.ShapeDtypeStruct((t, CONV_W), BF16)]
    for _ in range(4):
        out_specs.append(pl.BlockSpec((TM, QK_W), tile))
        out_shape.append(jax.ShapeDtypeStruct((t, QK_W), F32))
    out_specs.append(pl.BlockSpec((TM, LANES), tile))
    out_shape.append(jax.ShapeDtypeStruct((t, LANES), F32))
    return pl.pallas_call(
        functools.partial(_pre_kernel, row_len, tiles_per_batch, row0),
        grid=(t // TM,),
        in_specs=in_specs,
        out_specs=out_specs,
        out_shape=out_shape,
        compiler_params=_params("arbitrary"),
        name=f"pre_mixer_l{layer}_r{row_len}",
    )(*args)


def _gdn_kernel(has_s0, want_state, nblk, *refs):
    refs = list(refs)
    fwd_in = refs[0:4]
    bwd_in = refs[4:8]
    pos = 8
    s0_ref = None
    if has_s0:
        s0_ref = refs[pos]
        pos += 1
    of_ref, ob_ref = refs[pos], refs[pos + 1]
    pos += 2
    so_ref = None
    if want_state:
        so_ref = refs[pos]
        pos += 1
    s_scr, gf_scr, gb_scr = refs[pos:pos + 3]

    j = pl.program_id(1)

    @pl.when(j == 0)
    def _():
        if has_s0:
            s_scr[...] = s0_ref[0]
        else:
            s_scr[...] = jnp.zeros_like(s_scr)

    pc = lax.broadcasted_iota(jnp.int32, (TM, 1), 0) & (CHUNK - 1)
    g = fwd_in[3][0]
    for s in (1, 2, 4, 8, 16, 32):
        g = g + jnp.where(pc >= s, pltpu.roll(g, s, 0), 0.0)
    gf_scr[...] = g
    g = bwd_in[3][0]
    for s in (1, 2, 4, 8, 16, 32):
        g = g + jnp.where(pc < CHUNK - s, pltpu.roll(g, TM - s, 0), 0.0)
    gb_scr[...] = g

    ii = lax.broadcasted_iota(jnp.int32, (2 * CHUNK, 2 * CHUNK), 0)
    jj = lax.broadcasted_iota(jnp.int32, (2 * CHUNK, 2 * CHUNK), 1)
    blk = lambda n: (ii // n) == (jj // n)
    same = blk(CHUNK)
    eye = (ii == jj).astype(F32)
    masks = (
        ((same & (ii >= jj)).astype(F32), (same & (ii > jj)).astype(F32)),
        ((same & (ii <= jj)).astype(F32), (same & (ii < jj)).astype(F32)),
    )
    blk_diag = blk(INV_BASE).astype(F32)
    merge_masks = []
    n = INV_BASE
    while n < CHUNK:
        merge_masks.append((blk(2 * n) & jnp.logical_not(blk(n))).astype(F32))
        n *= 2

    def colb(arr, lane):
        return jnp.broadcast_to(arr[:, lane:lane + 1], (CHUNK, LANES))

    def stack2(a, b):
        return jnp.concatenate([a, b], axis=0)

    def mm(a, b):
        return _dot(a.astype(BF16), b.astype(BF16))

    n_chunks = TM // CHUNK
    n_pairs = DN_HEADS // 2
    hs = lambda hd: slice(hd * DK, (hd + 1) * DK)
    chains = [(d, c, p) for d in (0, 1) for c in range(n_chunks) for p in range(n_pairs)]
    st = {}

    def load_gates(key):
        d, c, p = key
        sm_ref = (fwd_in if d == 0 else bwd_in)[3]
        g_scr = gf_scr if d == 0 else gb_scr
        rows = slice(c * CHUNK, (c + 1) * CHUNK)
        gc = g_scr[rows, :]
        sm = sm_ref[0, rows, :]
        g2 = stack2(colb(gc, 2 * DN_HEADS + DN_HEADS * d + 2 * p),
                    colb(gc, 2 * DN_HEADS + DN_HEADS * d + 2 * p + 1))
        b2 = stack2(colb(sm, DN_HEADS * d + 2 * p), colb(sm, DN_HEADS * d + 2 * p + 1))
        return g2, b2

    def load_heads(key, ref):
        d, c, p = key
        rows = slice(c * CHUNK, (c + 1) * CHUNK)
        return stack2(ref[0, rows, hs(2 * p)], ref[0, rows, hs(2 * p + 1)])

    def inverse_operand(key):
        d = key[0]
        g2, b2 = load_gates(key)
        m_strict = masks[d][1]
        dec_strict = jnp.exp((g2 - g2.T) * m_strict) * m_strict
        k2b = load_heads(key, (fwd_in if d == 0 else bwd_in)[1]).astype(BF16)
        a = _dot_nt(k2b, k2b) * b2 * dec_strict
        st[key] = dict(a=a, dec_strict=dec_strict, pw=a * blk_diag)
        st[key]["t"] = eye - st[key]["pw"]

    def recurrence_operands(key):
        d = key[0]
        q_ref, k_ref, v_ref, _ = fwd_in if d == 0 else bwd_in
        g2, b2 = load_gates(key)
        q2, k2, v2 = load_heads(key, q_ref), load_heads(key, k_ref), load_heads(key, v_ref)
        eg = jnp.exp(g2)
        end_row = CHUNK - 1 if d == 0 else 0
        g_end = stack2(jnp.broadcast_to(g2[end_row:end_row + 1, :], (CHUNK, LANES)),
                       jnp.broadcast_to(g2[CHUNK + end_row:CHUNK + end_row + 1, :], (CHUNK, LANES)))
        c_ = st[key]
        c_["qk"] = (_dot_nt(q2.astype(BF16), k2.astype(BF16)) * (c_["dec_strict"] + eye)).astype(BF16)
        c_["rhs"] = jnp.concatenate([v2 * b2, k2 * (b2 * eg)], axis=1).astype(BF16)
        c_["qd"] = q2 * eg
        c_["k_end"] = (k2 * jnp.exp(g_end - g2)).astype(BF16)
        c_["decay"] = [jnp.exp(g2[hh * CHUNK + end_row: hh * CHUNK + end_row + 1, :]) for hh in (0, 1)]

    for key in chains:
        inverse_operand(key)
    def square(key):
        st[key]["pw"] = mm(st[key]["pw"], st[key]["pw"])

    def accumulate(key):
        st[key]["t"] = st[key]["t"] + mm(st[key]["t"], st[key]["pw"])

    stages = []
    lvl = 2
    while lvl < INV_BASE:
        stages += [square, accumulate]
        lvl *= 2
    for m_off in merge_masks:
        def off_times_t(key, m_off=m_off):
            st[key]["pw"] = mm(st[key]["a"] * m_off, st[key]["t"])

        def merge(key):
            st[key]["t"] = st[key]["t"] - mm(st[key]["t"], st[key]["pw"])

        stages += [off_times_t, merge]
    per_stage = -(-len(chains) // len(stages))
    pending = list(chains)
    for stage in stages:
        for key in chains:
            stage(key)
        for key in pending[:per_stage]:
            recurrence_operands(key)
        pending = pending[per_stage:]
    assert not pending
    for key in chains:
        st[key]["uw"] = _dot(st[key]["t"].astype(BF16), st[key]["rhs"])

    state = {(d, hd): s_scr[d, hd] for d in (0, 1) for hd in range(DN_HEADS)}
    for step in range(n_chunks):
        keys = [(d, step if d == 0 else n_chunks - 1 - step, p) for d in (0, 1) for p in range(n_pairs)]
        ws = {}
        for key in keys:
            d, c, p = key
            c_ = st[key]
            for hh in (0, 1):
                sl = slice(hh * CHUNK, (hh + 1) * CHUNK)
                lhs = stack2(c_["uw"][sl, DV:], c_["qd"][sl]).astype(BF16)
                ws[key, hh] = _dot(lhs, state[d, 2 * p + hh].astype(BF16))
        vn = {}
        for key in keys:
            d, c, p = key
            c_ = st[key]
            vn[key] = stack2(*[c_["uw"][hh * CHUNK:(hh + 1) * CHUNK, :DV] - ws[key, hh][:CHUNK]
                               for hh in (0, 1)]).astype(BF16)
            o2 = stack2(ws[key, 0][CHUNK:], ws[key, 1][CHUNK:]) + _dot(c_["qk"], vn[key])
            o_ref = of_ref if d == 0 else ob_ref
            for hh in (0, 1):
                o_ref[0, c * CHUNK:(c + 1) * CHUNK, hs(2 * p + hh)] = o2[hh * CHUNK:(hh + 1) * CHUNK]
        for key in keys:
            d, c, p = key
            c_ = st[key]
            for hh in (0, 1):
                sl = slice(hh * CHUNK, (hh + 1) * CHUNK)
                state[d, 2 * p + hh] = (state[d, 2 * p + hh] * c_["decay"][hh]
                                        + _dot_tn(c_["k_end"][sl], vn[key][sl]))
    for (d, hd), val in state.items():
        s_scr[d, hd] = val

    if want_state:
        @pl.when(j == nblk - 1)
        def _():
            so_ref[0] = s_scr[...]


def _gdn(q, k, v, small, s0, want_state, name):
    b, l, _ = q.shape
    nblk = l // TM
    fwd = lambda bi, j: (bi, j, 0)
    bwd = lambda bi, j: (bi, nblk - 1 - j, 0)
    st = lambda bi, j: (bi, 0, 0, 0, 0)
    in_specs, args = [], []
    for imap in (fwd, bwd):
        in_specs += [pl.BlockSpec((1, TM, QK_W), imap)] * 3 + [pl.BlockSpec((1, TM, LANES), imap)]
        args += [q, k, v, small]
    if s0 is not None:
        in_specs.append(pl.BlockSpec((1, 2, DN_HEADS, DK, DV), st))
        args.append(s0)
    out_specs = [pl.BlockSpec((1, TM, V_W), fwd), pl.BlockSpec((1, TM, V_W), bwd)]
    out_shape = [jax.ShapeDtypeStruct((b, l, V_W), F32)] * 2
    if want_state:
        out_specs.append(pl.BlockSpec((1, 2, DN_HEADS, DK, DV), st))
        out_shape.append(jax.ShapeDtypeStruct((b, 2, DN_HEADS, DK, DV), F32))
    return pl.pallas_call(
        functools.partial(_gdn_kernel, s0 is not None, want_state, nblk),
        grid=(b, nblk),
        in_specs=in_specs,
        out_specs=out_specs,
        out_shape=out_shape,
        scratch_shapes=[
            pltpu.VMEM((2, DN_HEADS, DK, DV), F32),
            pltpu.VMEM((TM, LANES), F32),
            pltpu.VMEM((TM, LANES), F32),
        ],
        compiler_params=_params("arbitrary", "arbitrary"),
        name=name,
    )(*args)


def _post_kernel(tiles_per_batch, row0, x_ref, of_ref, ob_ref, g_ref, yc_ref, mod_ref, onw_ref,
                 wo_ref, n2_ref, xo_ref, h2_ref):
    row = row0 + pl.program_id(0) // tiles_per_batch if tiles_per_batch else row0
    o = of_ref[...] + ob_ref[...]
    gated = []
    for hd in range(DN_HEADS):
        oh = o[:, hd * DV:(hd + 1) * DV]
        on = oh * lax.rsqrt(_mean_sq(oh) + EPS) * onw_ref[0]
        gated.append((on * g_ref[:, hd * DV:(hd + 1) * DV]).astype(BF16))
    og = jnp.concatenate(gated, axis=1)
    y = _dot(yc_ref[...], wo_ref[0, 0:CONV_W, :]) + _dot(og, wo_ref[0, CONV_W:, :])
    x = x_ref[...] + _mod_row(mod_ref, 2, row) * y
    xo_ref[...] = x
    xn = x * lax.rsqrt(_mean_sq(x) + EPS)
    h2 = xn * (n2_ref[0] * (1.0 + _mod_row(mod_ref, 4, row))) + _mod_row(mod_ref, 3, row)
    h2_ref[...] = h2.astype(BF16)


def _post_mixer(layer, x, o_f, o_b, gate, yconv, mod, tiles_per_batch, row0, w, tag):
    t = x.shape[0]
    tile = lambda i: (i, 0)
    lay3 = lambda i: (layer, 0, 0)
    return pl.pallas_call(
        functools.partial(_post_kernel, tiles_per_batch, row0),
        grid=(t // TM,),
        in_specs=[
            pl.BlockSpec((TM, D_MODEL), tile),
            pl.BlockSpec((TM, V_W), tile),
            pl.BlockSpec((TM, V_W), tile),
            pl.BlockSpec((TM, V_W), tile),
            pl.BlockSpec((TM, CONV_W), tile),
            pl.BlockSpec((1, 6, MOD_ROWS, D_MODEL), lambda i: (layer, 0, 0, 0)),
            pl.BlockSpec((1, 1, DV), lay3),
            pl.BlockSpec((1, D_MODEL, D_MODEL), lay3),
            pl.BlockSpec((1, 1, D_MODEL), lay3),
        ],
        out_specs=[pl.BlockSpec((TM, D_MODEL), tile), pl.BlockSpec((TM, D_MODEL), tile)],
        out_shape=[jax.ShapeDtypeStruct((t, D_MODEL), F32), jax.ShapeDtypeStruct((t, D_MODEL), BF16)],
        compiler_params=_params("arbitrary"),
        name=f"post_mixer_l{layer}_{tag}",
    )(x, o_f, o_b, gate, yconv, mod, w["o_norm_w"], w["w_out"], w["norm2_w"])


def _ffn_kernel(row_len, multi_row, block_rows, row0, final, h_ref, x_ref, mod_ref, fw_ref, wu_ref,
                wg_ref, cw_ref, wd_ref, o_ref, u_scr, g_scr, a_scr):
    part = TB // FFN_PARTS
    n_sub = part // CONV_ROWS

    @pl.when(pl.program_id(1) == 0)
    def _():
        o_ref[...] = jnp.zeros_like(o_ref)

    def up_proj(p):
        rows = slice(p * part, (p + 1) * part)
        hb = h_ref[rows, :]
        u_scr[rows, :] = _dot(hb, wu_ref[0])
        g_scr[rows, :] = _dot(hb, wg_ref[0])

    tok = lax.broadcasted_iota(jnp.int32, (CONV_ROWS, 1), 0)

    def taps(r0, lanes):
        g = g_scr[r0:r0 + CONV_ROWS, lanes]
        if r0 % row_len == 0:
            gm = jnp.where(tok == 0, 0.0, pltpu.roll(g, 1, 0))
        else:
            win = g_scr[r0 - 8:r0 + CONV_ROWS, lanes]
            gm = pltpu.roll(win, 1, 0)[8:]
        if (r0 + CONV_ROWS) % row_len == 0:
            gp = jnp.where(tok == CONV_ROWS - 1, 0.0, pltpu.roll(g, CONV_ROWS - 1, 0))
        else:
            win = g_scr[r0:r0 + CONV_ROWS + 8, lanes]
            gp = pltpu.roll(win, CONV_ROWS + 7, 0)[:CONV_ROWS]
        return gm, g, gp

    def conv_gate(p):
        for half in range(FB // LANES):
            lanes = slice(half * LANES, (half + 1) * LANES)
            cw = [cw_ref[0, i:i + 1, lanes] for i in range(9)]
            cache = {}
            for sb in range(p * n_sub, (p + 1) * n_sub):
                r0 = sb * CONV_ROWS
                conv = None
                for ky in (0, 1, 2) if multi_row else (1,):
                    src = r0 + (ky - 1) * row_len
                    if src < 0 or src >= TB:
                        continue
                    if src not in cache:
                        cache[src] = taps(src, lanes)
                    gm, g, gp = cache[src]
                    term = cw[3 * ky] * gm + cw[3 * ky + 1] * g + cw[3 * ky + 2] * gp
                    conv = term if conv is None else conv + term
                act = jax.nn.gelu(conv) * u_scr[r0:r0 + CONV_ROWS, lanes]
                a_scr[r0:r0 + CONV_ROWS, lanes] = act.astype(BF16)

    def down_proj(p):
        rows = slice(p * part, (p + 1) * part)
        o_ref[rows, :] += _dot(a_scr[rows, :], wd_ref[0])

    up_proj(0)
    for p in range(FFN_PARTS):
        if p + 1 < FFN_PARTS:
            up_proj(p + 1)
        if p > 0:
            down_proj(p - 1)
        conv_gate(p)
    down_proj(FFN_PARTS - 1)

    @pl.when(pl.program_id(1) == pl.num_programs(1) - 1)
    def _():
        row = row0 + pl.program_id(0) if block_rows else row0
        gate2 = _mod_row(mod_ref, 5, row)

        def finish(c, carry):
            rows = pl.ds(pl.multiple_of(c * TM, TM), TM)
            x = x_ref[rows, :] + gate2 * o_ref[rows, :]
            if final:
                x = x * lax.rsqrt(_mean_sq(x) + EPS) * fw_ref[...]
            o_ref[rows, :] = x
            return carry

        lax.fori_loop(0, TB // TM, finish, 0)


def _conv_glu(layer, h2, x, mod, row_len, multi_row, block_rows, row0, w, tag):
    t = h2.shape[0]
    nf = D_FF // FB
    return pl.pallas_call(
        functools.partial(_ffn_kernel, row_len, multi_row, block_rows, row0, layer == DEPTH - 1),
        grid=(t // TB, nf),
        in_specs=[
            pl.BlockSpec((TB, D_MODEL), lambda i, f: (i, 0)),
            pl.BlockSpec((TB, D_MODEL), lambda i, f: (i, 0)),
            pl.BlockSpec((1, 6, MOD_ROWS, D_MODEL), lambda i, f: (layer, 0, 0, 0)),
            pl.BlockSpec((1, D_MODEL), lambda i, f: (0, 0)),
            pl.BlockSpec((1, D_MODEL, FB), lambda i, f: (layer, 0, f)),
            pl.BlockSpec((1, D_MODEL, FB), lambda i, f: (layer, 0, nf + f)),
            pl.BlockSpec((1, 9, FB), lambda i, f: (layer, 0, f)),
            pl.BlockSpec((1, FB, D_MODEL), lambda i, f: (layer, f, 0)),
        ],
        out_specs=pl.BlockSpec((TB, D_MODEL), lambda i, f: (i, 0)),
        out_shape=jax.ShapeDtypeStruct((t, D_MODEL), F32),
        scratch_shapes=[
            pltpu.VMEM((TB, FB), F32),
            pltpu.VMEM((TB, FB), F32),
            pltpu.VMEM((TB, FB), BF16),
        ],
        compiler_params=_params("arbitrary", "arbitrary"),
        name=f"conv_glu_l{layer}_{tag}",
    )(h2, x, mod, w["final_norm_w"], w["w_up"], w["w_up"], w["ffn_conv_w"], w["w_down"])


def kernel(x_prompt, x_sample, state_deltanet, c, c_ctx, w_ada, b_ada, norm1_w, w_in, conv_a_w,
           qkv_conv_w, a_log, dt_bias, o_norm_w, w_out, norm2_w, w_up, ffn_conv_w, w_down,
           final_norm_w):
    b_ctx, ctx_len, _ = x_prompt.shape
    b_lat, lat_len, _ = x_sample.shape
    assert ctx_len == TM and lat_len == TB and lat_len % GRID_W == 0 and b_lat + 1 <= MOD_ROWS
    assert (b_ctx * ctx_len) % TB == 0

    c_all = jnp.concatenate(
        [c_ctx[None, :], c, jnp.zeros((MOD_ROWS - 1 - b_lat, D_MODEL), F32)], axis=0)
    mod = _modulation(c_all, w_ada, b_ada)

    gate_rows = jnp.stack([a_log.reshape(DEPTH, -1), dt_bias.reshape(DEPTH, -1)], axis=1)
    gate_params = jnp.pad(gate_rows, ((0, 0), (0, 6), (2 * DN_HEADS, LANES - N_GATE_COLS)))
    w = {
        "norm1_w": norm1_w.reshape(DEPTH, 1, D_MODEL),
        "w_main": w_in[:, :, :MAIN_COLS].astype(BF16),
        "w_small": jnp.pad(w_in[:, :, MAIN_COLS:], ((0, 0), (0, 0), (0, LANES - N_GATE_COLS))).astype(BF16),
        "conv_a_w": conv_a_w,
        "qkv_conv_w": qkv_conv_w,
        "gate_params": gate_params,
        "o_norm_w": o_norm_w.reshape(DEPTH, 1, DV),
        "w_out": w_out.astype(BF16),
        "norm2_w": norm2_w.reshape(DEPTH, 1, D_MODEL),
        "w_up": w_up.astype(BF16),
        "ffn_conv_w": ffn_conv_w.reshape(DEPTH, 9, D_FF),
        "w_down": w_down.astype(BF16),
        "final_norm_w": final_norm_w.reshape(1, D_MODEL),
    }

    streams = {
        "ctx": dict(x=x_prompt.reshape(b_ctx * ctx_len, D_MODEL), b=b_ctx, l=ctx_len, row_len=ctx_len,
                    tpb=0, row0=0),
        "lat": dict(x=x_sample.reshape(b_lat * lat_len, D_MODEL), b=b_lat, l=lat_len, row_len=GRID_W,
                    tpb=lat_len // TM, row0=1),
    }
    ctx_states = []
    for layer in range(DEPTH):
        for tag, s in streams.items():
            yconv, q, k, v, gate, small = _pre_mixer(layer, s["x"], mod, s["row_len"], s["tpb"],
                                                     s["row0"], w)
            shp = lambda a: a.reshape(s["b"], s["l"], a.shape[-1])
            if tag == "ctx":
                o_f, o_b, s_fin = _gdn(shp(q), shp(k), shp(v), shp(small), None, True,
                                       f"gdn_l{layer}_ctx")
                ctx_states.append(s_fin)
            else:
                o_f, o_b = _gdn(shp(q), shp(k), shp(v), shp(small), state_deltanet[:, layer], False,
                                f"gdn_l{layer}_lat")
            flat = lambda a: a.reshape(s["b"] * s["l"], a.shape[-1])
            x_mid, h2 = _post_mixer(layer, s["x"], flat(o_f), flat(o_b), gate, yconv, mod, s["tpb"],
                                    s["row0"], w, tag)
            s["x"] = _conv_glu(layer, h2, x_mid, mod, s["row_len"], s["l"] // s["row_len"] > 1,
                               tag == "lat", s["row0"], w, tag)
    outs = {tag: s["x"].reshape(s["b"], s["l"], D_MODEL) for tag, s in streams.items()}
    new_state = jnp.stack(ctx_states, axis=1)
    return (outs["ctx"], outs["lat"], new_state)
```

```python
import functools

import jax
import jax.numpy as jnp
from jax import lax
from jax.experimental import pallas as pl
from jax.experimental.pallas import tpu as pltpu

D_MODEL = 1024
DEPTH = 2
GRID_W = 64
CONV_W = D_MODEL // 2
DN_HEADS = 4
DK = 128
DV = 128
QK_W = DN_HEADS * DK
V_W = DN_HEADS * DV
D_FF = 2816
CHUNK = 64
EPS = 1e-6
MAIN_COLS = 3 * CONV_W + 2 * QK_W + 2 * V_W
N_GATE_COLS = 4 * DN_HEADS

LANES = 128
TM = 256
TB = 2048
FB = 256
FFN_PARTS = 4
CONV_ROWS = 64
INV_BASE = 16
MOD_ROWS = 16
VMEM_LIMIT = 56 * 1024 * 1024

BF16 = jnp.bfloat16
F32 = jnp.float32


def _dot(a, b):
    return jnp.dot(a, b, preferred_element_type=F32)


def _dot_nt(a, b):
    return lax.dot_general(a, b, (((1,), (1,)), ((), ())), preferred_element_type=F32)


def _dot_tn(a, b):
    return lax.dot_general(a, b, (((0,), (0,)), ((), ())), preferred_element_type=F32)


def _dot_split(a, b):
    ah = a.astype(BF16)
    al = (a - ah.astype(F32)).astype(BF16)
    bh = b.astype(BF16)
    bl = (b - bh.astype(F32)).astype(BF16)
    return _dot(ah, bh) + (_dot(ah, bl) + _dot(al, bh))


def _silu(x):
    return x * jax.nn.sigmoid(x)


def _mean_sq(x):
    return jnp.mean(x * x, axis=-1, keepdims=True)


def _params(*sem):
    return pltpu.CompilerParams(dimension_semantics=sem, vmem_limit_bytes=VMEM_LIMIT)


def _mod_kernel(c_ref, w_ref, b_ref, o_ref):
    a = _silu(c_ref[...]).astype(BF16)
    o_ref[0, 0] = _dot(a, w_ref[0].astype(BF16)) + b_ref[0]


def _modulation(c_all, w_ada, b_ada):
    return pl.pallas_call(
        _mod_kernel,
        grid=(DEPTH, 6),
        in_specs=[
            pl.BlockSpec((MOD_ROWS, D_MODEL), lambda l, n: (0, 0)),
            pl.BlockSpec((1, D_MODEL, D_MODEL), lambda l, n: (l, 0, n)),
            pl.BlockSpec((1, 1, D_MODEL), lambda l, n: (l, 0, n)),
        ],
        out_specs=pl.BlockSpec((1, 1, MOD_ROWS, D_MODEL), lambda l, n: (l, n, 0, 0)),
        out_shape=jax.ShapeDtypeStruct((DEPTH, 6, MOD_ROWS, D_MODEL), F32),
        compiler_params=_params("arbitrary", "arbitrary"),
        name="modulation",
    )(c_all, w_ada, b_ada.reshape(DEPTH, 1, 6 * D_MODEL))


def _mod_row(mod_ref, which, row):
    return mod_ref[0, which, pl.ds(row, 1), :]


def _conv3(u, w_ref, lo, first, last):
    n = u.shape[0]
    width = u.shape[1]
    up = jnp.where(first, 0.0, pltpu.roll(u, 1, 0))
    un = jnp.where(last, 0.0, pltpu.roll(u, n - 1, 0))
    w0 = w_ref[0, 0:1, lo:lo + width]
    w1 = w_ref[0, 1:2, lo:lo + width]
    w2 = w_ref[0, 2:3, lo:lo + width]
    return w0 * up + w1 * u + w2 * un


def _pre_kernel(row_len, tiles_per_batch, row0, x_ref, mod_ref, n1_ref, wm_ref, ws_ref, cw_ref,
                qw_ref, gp_ref, yc_ref, q_ref, k_ref, v_ref, g_ref, sm_ref):
    row = row0 + pl.program_id(0) // tiles_per_batch if tiles_per_batch else row0
    x = x_ref[...]
    xn = x * lax.rsqrt(_mean_sq(x) + EPS)
    h = xn * (n1_ref[0] * (1.0 + _mod_row(mod_ref, 1, row))) + _mod_row(mod_ref, 0, row)
    hb = h.astype(BF16)

    pos = lax.broadcasted_iota(jnp.int32, (TM, 1), 0) & (row_len - 1)
    first = pos == 0
    last = pos == row_len - 1

    cb = _dot(hb, wm_ref[0, :, 0:CONV_W])
    cc = _dot(hb, wm_ref[0, :, CONV_W:2 * CONV_W])
    cx = _dot(hb, wm_ref[0, :, 2 * CONV_W:3 * CONV_W])
    yc_ref[...] = (cb * _conv3(cc * cx, cw_ref, 0, first, last)).astype(BF16)

    base = 3 * CONV_W
    for idx, out_ref in enumerate((q_ref, k_ref, v_ref)):
        z = _dot(hb, wm_ref[0, :, base + idx * QK_W: base + (idx + 1) * QK_W])
        a = _silu(_conv3(z, qw_ref, idx * QK_W, first, last))
        if idx < 2:
            scale = DK ** -0.5 if idx == 0 else 1.0
            for hd in range(DN_HEADS):
                ah = a[:, hd * DK:(hd + 1) * DK]
                ss = jnp.sum(ah * ah, axis=-1, keepdims=True)
                out_ref[:, hd * DK:(hd + 1) * DK] = ah * (lax.rsqrt(ss + EPS) * scale)
        else:
            out_ref[...] = a
    g_ref[...] = _silu(_dot(hb, wm_ref[0, :, base + 3 * QK_W: base + 3 * QK_W + V_W]))

    zs = _dot(hb, ws_ref[0])
    lane = lax.broadcasted_iota(jnp.int32, (TM, LANES), 1)
    a_log = gp_ref[0, 0:1, :]
    dt_bias = gp_ref[0, 1:2, :]
    log_alpha = -jnp.exp(a_log) * jax.nn.softplus(zs + dt_bias)
    sm_ref[...] = jnp.where(lane < 2 * DN_HEADS, jax.nn.sigmoid(zs),
                            jnp.where(lane < N_GATE_COLS, log_alpha, 0.0))


def _pre_mixer(layer, x, mod, row_len, tiles_per_batch, row0, w):
    t = x.shape[0]
    tile = lambda i: (i, 0)
    lay3 = lambda i: (layer, 0, 0)
    in_specs = [
        pl.BlockSpec((TM, D_MODEL), tile),
        pl.BlockSpec((1, 6, MOD_ROWS, D_MODEL), lambda i: (layer, 0, 0, 0)),
        pl.BlockSpec((1, 1, D_MODEL), lay3),
        pl.BlockSpec((1, D_MODEL, MAIN_COLS), lay3),
        pl.BlockSpec((1, D_MODEL, LANES), lay3),
        pl.BlockSpec((1, 3, CONV_W), lay3),
        pl.BlockSpec((1, 3, 2 * QK_W + V_W), lay3),
        pl.BlockSpec((1, 8, LANES), lay3),
    ]
    args = [x, mod, w["norm1_w"], w["w_main"], w["w_small"], w["conv_a_w"], w["qkv_conv_w"],
            w["gate_params"]]
    out_specs = [pl.BlockSpec((TM, CONV_W), tile)]
    out_shape = [jax.ShapeDtypeStruct((t, CONV_W), BF16)]
    for _ in range(4):
        out_specs.append(pl.BlockSpec((TM, QK_W), tile))
        out_shape.append(jax.ShapeDtypeStruct((t, QK_W), F32))
    out_specs.append(pl.BlockSpec((TM, LANES), tile))
    out_shape.append(jax.ShapeDtypeStruct((t, LANES), F32))
    return pl.pallas_call(
        functools.partial(_pre_kernel, row_len, tiles_per_batch, row0),
        grid=(t // TM,),
        in_specs=in_specs,
        out_specs=out_specs,
        out_shape=out_shape,
        compiler_params=_params("arbitrary"),
        name=f"pre_mixer_l{layer}_r{row_len}",
    )(*args)


def _gdn_kernel(has_s0, want_state, nblk, *refs):
    refs = list(refs)
    fwd_in = refs[0:4]
    bwd_in = refs[4:8]
    pos = 8
    s0_ref = None
    if has_s0:
        s0_ref = refs[pos]
        pos += 1
    of_ref, ob_ref = refs[pos], refs[pos + 1]
    pos += 2
    so_ref = None
    if want_state:
        so_ref = refs[pos]
        pos += 1
    s_scr, gf_scr, gb_scr = refs[pos:pos + 3]

    j = pl.program_id(1)

    @pl.when(j == 0)
    def _():
        if has_s0:
            s_scr[...] = s0_ref[0, 0]
        else:
            s_scr[...] = jnp.zeros_like(s_scr)

    pc = lax.broadcasted_iota(jnp.int32, (TM, 1), 0) & (CHUNK - 1)
    g = fwd_in[3][0]
    for s in (1, 2, 4, 8, 16, 32):
        g = g + jnp.where(pc >= s, pltpu.roll(g, s, 0), 0.0)
    gf_scr[...] = g
    g = bwd_in[3][0]
    for s in (1, 2, 4, 8, 16, 32):
        g = g + jnp.where(pc < CHUNK - s, pltpu.roll(g, TM - s, 0), 0.0)
    gb_scr[...] = g

    ii = lax.broadcasted_iota(jnp.int32, (2 * CHUNK, 2 * CHUNK), 0)
    jj = lax.broadcasted_iota(jnp.int32, (2 * CHUNK, 2 * CHUNK), 1)
    blk = lambda n: (ii // n) == (jj // n)
    same = blk(CHUNK)
    eye = (ii == jj).astype(F32)
    masks = (
        ((same & (ii >= jj)).astype(F32), (same & (ii > jj)).astype(F32)),
        ((same & (ii <= jj)).astype(F32), (same & (ii < jj)).astype(F32)),
    )
    blk_diag = blk(INV_BASE).astype(F32)
    merge_masks = []
    n = INV_BASE
    while n < CHUNK:
        merge_masks.append((blk(2 * n) & jnp.logical_not(blk(n))).astype(F32))
        n *= 2

    def colb(arr, lane):
        return jnp.broadcast_to(arr[:, lane:lane + 1], (CHUNK, LANES))

    def stack2(a, b):
        return jnp.concatenate([a, b], axis=0)

    def mm(a, b):
        return _dot(a.astype(BF16), b.astype(BF16))

    n_chunks = TM // CHUNK
    n_pairs = DN_HEADS // 2
    hs = lambda hd: slice(hd * DK, (hd + 1) * DK)
    chains = [(d, c, p) for d in (0, 1) for c in range(n_chunks) for p in range(n_pairs)]
    st = {}
    for key in chains:
        d, c, p = key
        q_ref, k_ref, v_ref, sm_ref = fwd_in if d == 0 else bwd_in
        g_scr = gf_scr if d == 0 else gb_scr
        rows = slice(c * CHUNK, (c + 1) * CHUNK)
        h0, h1 = 2 * p, 2 * p + 1
        gc = g_scr[rows, :]
        sm = sm_ref[0, rows, :]
        g2 = stack2(colb(gc, 2 * DN_HEADS + DN_HEADS * d + h0),
                    colb(gc, 2 * DN_HEADS + DN_HEADS * d + h1))
        b2 = stack2(colb(sm, DN_HEADS * d + h0), colb(sm, DN_HEADS * d + h1))
        q2 = stack2(q_ref[0, rows, hs(h0)], q_ref[0, rows, hs(h1)])
        k2 = stack2(k_ref[0, rows, hs(h0)], k_ref[0, rows, hs(h1)])
        v2 = stack2(v_ref[0, rows, hs(h0)], v_ref[0, rows, hs(h1)])
        m_incl, m_strict = masks[d]
        e = jnp.exp((g2 - g2.T) * m_incl)
        eg = jnp.exp(g2)
        end_row = CHUNK - 1 if d == 0 else 0
        g_end = stack2(jnp.broadcast_to(g2[end_row:end_row + 1, :], (CHUNK, LANES)),
                       jnp.broadcast_to(g2[CHUNK + end_row:CHUNK + end_row + 1, :], (CHUNK, LANES)))
        st[key] = dict(
            k2b=k2.astype(BF16), q2b=q2.astype(BF16), b2=b2,
            dec_strict=e * m_strict, dec_incl=e * m_incl,
            rhs=jnp.concatenate([v2 * b2, k2 * (b2 * eg)], axis=1).astype(BF16),
            qd=q2 * eg, k_end=(k2 * jnp.exp(g_end - g2)).astype(BF16),
            decay=[jnp.exp(g2[hh * CHUNK + end_row: hh * CHUNK + end_row + 1, :]) for hh in (0, 1)],
        )
    for key in chains:
        c_ = st[key]
        c_["a"] = _dot_nt(c_["k2b"], c_["k2b"]) * c_["b2"] * c_["dec_strict"]
        c_["qk"] = (_dot_nt(c_["q2b"], c_["k2b"]) * c_["dec_incl"]).astype(BF16)
    for key in chains:
        c_ = st[key]
        c_["pw"] = c_["a"] * blk_diag
        c_["t"] = eye - c_["pw"]
    lvl = 2
    while lvl < INV_BASE:
        for key in chains:
            st[key]["pw"] = mm(st[key]["pw"], st[key]["pw"])
        for key in chains:
            st[key]["t"] = st[key]["t"] + mm(st[key]["t"], st[key]["pw"])
        lvl *= 2
    for m_off in merge_masks:
        for key in chains:
            st[key]["pw"] = mm(st[key]["a"] * m_off, st[key]["t"])
        for key in chains:
            st[key]["t"] = st[key]["t"] - mm(st[key]["t"], st[key]["pw"])
    for key in chains:
        st[key]["uw"] = _dot(st[key]["t"].astype(BF16), st[key]["rhs"])

    state = {(d, hd): s_scr[d, hd] for d in (0, 1) for hd in range(DN_HEADS)}
    for step in range(n_chunks):
        keys = [(d, step if d == 0 else n_chunks - 1 - step, p) for d in (0, 1) for p in range(n_pairs)]
        ws = {}
        for key in keys:
            d, c, p = key
            c_ = st[key]
            for hh in (0, 1):
                sl = slice(hh * CHUNK, (hh + 1) * CHUNK)
                lhs = stack2(c_["uw"][sl, DV:], c_["qd"][sl]).astype(BF16)
                ws[key, hh] = _dot(lhs, state[d, 2 * p + hh].astype(BF16))
        vn = {}
        for key in keys:
            d, c, p = key
            c_ = st[key]
            vn[key] = stack2(*[c_["uw"][hh * CHUNK:(hh + 1) * CHUNK, :DV] - ws[key, hh][:CHUNK]
                               for hh in (0, 1)]).astype(BF16)
            o2 = stack2(ws[key, 0][CHUNK:], ws[key, 1][CHUNK:]) + _dot(c_["qk"], vn[key])
            o_ref = of_ref if d == 0 else ob_ref
            for hh in (0, 1):
                o_ref[0, c * CHUNK:(c + 1) * CHUNK, hs(2 * p + hh)] = o2[hh * CHUNK:(hh + 1) * CHUNK]
        for key in keys:
            d, c, p = key
            c_ = st[key]
            for hh in (0, 1):
                sl = slice(hh * CHUNK, (hh + 1) * CHUNK)
                state[d, 2 * p + hh] = (state[d, 2 * p + hh] * c_["decay"][hh]
                                        + _dot_tn(c_["k_end"][sl], vn[key][sl]))
    for (d, hd), val in state.items():
        s_scr[d, hd] = val

    if want_state:
        @pl.when(j == nblk - 1)
        def _():
            so_ref[0] = s_scr[...]


def _gdn(q, k, v, small, s0, layer, want_state, name):
    b, l, _ = q.shape
    nblk = l // TM
    fwd = lambda bi, j: (bi, j, 0)
    bwd = lambda bi, j: (bi, nblk - 1 - j, 0)
    st = lambda bi, j: (bi, 0, 0, 0, 0)
    in_specs, args = [], []
    for imap in (fwd, bwd):
        in_specs += [pl.BlockSpec((1, TM, QK_W), imap)] * 3 + [pl.BlockSpec((1, TM, LANES), imap)]
        args += [q, k, v, small]
    if s0 is not None:
        in_specs.append(pl.BlockSpec((1, 1, 2, DN_HEADS, DK, DV), lambda bi, j: (bi, layer, 0, 0, 0, 0)))
        args.append(s0)
    out_specs = [pl.BlockSpec((1, TM, V_W), fwd), pl.BlockSpec((1, TM, V_W), bwd)]
    out_shape = [jax.ShapeDtypeStruct((b, l, V_W), F32)] * 2
    if want_state:
        out_specs.append(pl.BlockSpec((1, 2, DN_HEADS, DK, DV), st))
        out_shape.append(jax.ShapeDtypeStruct((b, 2, DN_HEADS, DK, DV), F32))
    return pl.pallas_call(
        functools.partial(_gdn_kernel, s0 is not None, want_state, nblk),
        grid=(b, nblk),
        in_specs=in_specs,
        out_specs=out_specs,
        out_shape=out_shape,
        scratch_shapes=[
            pltpu.VMEM((2, DN_HEADS, DK, DV), F32),
            pltpu.VMEM((TM, LANES), F32),
            pltpu.VMEM((TM, LANES), F32),
        ],
        compiler_params=_params("arbitrary", "arbitrary"),
        name=name,
    )(*args)


def _post_kernel(tiles_per_batch, row0, x_ref, of_ref, ob_ref, g_ref, yc_ref, mod_ref, onw_ref,
                 wo_ref, n2_ref, xo_ref, h2_ref):
    row = row0 + pl.program_id(0) // tiles_per_batch if tiles_per_batch else row0
    o = of_ref[...] + ob_ref[...]
    gated = []
    for hd in range(DN_HEADS):
        oh = o[:, hd * DV:(hd + 1) * DV]
        on = oh * lax.rsqrt(_mean_sq(oh) + EPS) * onw_ref[0]
        gated.append((on * g_ref[:, hd * DV:(hd + 1) * DV]).astype(BF16))
    og = jnp.concatenate(gated, axis=1)
    y = _dot(yc_ref[...], wo_ref[0, 0:CONV_W, :]) + _dot(og, wo_ref[0, CONV_W:, :])
    x = x_ref[...] + _mod_row(mod_ref, 2, row) * y
    xo_ref[...] = x
    xn = x * lax.rsqrt(_mean_sq(x) + EPS)
    h2 = xn * (n2_ref[0] * (1.0 + _mod_row(mod_ref, 4, row))) + _mod_row(mod_ref, 3, row)
    h2_ref[...] = h2.astype(BF16)


def _post_mixer(layer, x, o_f, o_b, gate, yconv, mod, tiles_per_batch, row0, w, tag):
    t = x.shape[0]
    tile = lambda i: (i, 0)
    lay3 = lambda i: (layer, 0, 0)
    return pl.pallas_call(
        functools.partial(_post_kernel, tiles_per_batch, row0),
        grid=(t // TM,),
        in_specs=[
            pl.BlockSpec((TM, D_MODEL), tile),
            pl.BlockSpec((TM, V_W), tile),
            pl.BlockSpec((TM, V_W), tile),
            pl.BlockSpec((TM, V_W), tile),
            pl.BlockSpec((TM, CONV_W), tile),
            pl.BlockSpec((1, 6, MOD_ROWS, D_MODEL), lambda i: (layer, 0, 0, 0)),
            pl.BlockSpec((1, 1, DV), lay3),
            pl.BlockSpec((1, D_MODEL, D_MODEL), lay3),
            pl.BlockSpec((1, 1, D_MODEL), lay3),
        ],
        out_specs=[pl.BlockSpec((TM, D_MODEL), tile), pl.BlockSpec((TM, D_MODEL), tile)],
        out_shape=[jax.ShapeDtypeStruct((t, D_MODEL), F32), jax.ShapeDtypeStruct((t, D_MODEL), BF16)],
        compiler_params=_params("arbitrary"),
        name=f"post_mixer_l{layer}_{tag}",
    )(x, o_f, o_b, gate, yconv, mod, w["o_norm_w"], w["w_out"], w["norm2_w"])


def _ffn_kernel(row_len, multi_row, block_rows, row0, final, h_ref, x_ref, mod_ref, fw_ref, wug_ref,
                cw_ref, wd_ref, o_ref, u_scr, g_scr, a_scr):
    part = TB // FFN_PARTS
    n_sub = part // CONV_ROWS

    @pl.when(pl.program_id(1) == 0)
    def _():
        o_ref[...] = jnp.zeros_like(o_ref)

    def up_proj(p):
        rows = slice(p * part, (p + 1) * part)
        hb = h_ref[rows, :]
        ug = _dot(hb, wug_ref[0, 0])
        u_scr[rows, :] = ug[:, :FB]
        g_scr[rows, :] = ug[:, FB:]

    tok = lax.broadcasted_iota(jnp.int32, (CONV_ROWS, 1), 0)

    def taps(r0, lanes):
        g = g_scr[r0:r0 + CONV_ROWS, lanes]
        if r0 % row_len == 0:
            gm = jnp.where(tok == 0, 0.0, pltpu.roll(g, 1, 0))
        else:
            win = g_scr[r0 - 8:r0 + CONV_ROWS, lanes]
            gm = pltpu.roll(win, 1, 0)[8:]
        if (r0 + CONV_ROWS) % row_len == 0:
            gp = jnp.where(tok == CONV_ROWS - 1, 0.0, pltpu.roll(g, CONV_ROWS - 1, 0))
        else:
            win = g_scr[r0:r0 + CONV_ROWS + 8, lanes]
            gp = pltpu.roll(win, CONV_ROWS + 7, 0)[:CONV_ROWS]
        return gm, g, gp

    def conv_gate(p):
        for half in range(FB // LANES):
            lanes = slice(half * LANES, (half + 1) * LANES)
            cw = [cw_ref[0, i:i + 1, lanes] for i in range(9)]
            cache = {}
            for sb in range(p * n_sub, (p + 1) * n_sub):
                r0 = sb * CONV_ROWS
                conv = None
                for ky in (0, 1, 2) if multi_row else (1,):
                    src = r0 + (ky - 1) * row_len
                    if src < 0 or src >= TB:
                        continue
                    if src not in cache:
                        cache[src] = taps(src, lanes)
                    gm, g, gp = cache[src]
                    term = cw[3 * ky] * gm + cw[3 * ky + 1] * g + cw[3 * ky + 2] * gp
                    conv = term if conv is None else conv + term
                act = jax.nn.gelu(conv) * u_scr[r0:r0 + CONV_ROWS, lanes]
                a_scr[r0:r0 + CONV_ROWS, lanes] = act.astype(BF16)

    def down_proj(p):
        rows = slice(p * part, (p + 1) * part)
        o_ref[rows, :] += _dot(a_scr[rows, :], wd_ref[0])

    up_proj(0)
    for p in range(FFN_PARTS):
        if p + 1 < FFN_PARTS:
            up_proj(p + 1)
        if p > 0:
            down_proj(p - 1)
        conv_gate(p)
    down_proj(FFN_PARTS - 1)

    @pl.when(pl.program_id(1) == pl.num_programs(1) - 1)
    def _():
        row = row0 + pl.program_id(0) if block_rows else row0
        gate2 = _mod_row(mod_ref, 5, row)

        def finish(c, carry):
            rows = pl.ds(pl.multiple_of(c * TM, TM), TM)
            x = x_ref[rows, :] + gate2 * o_ref[rows, :]
            if final:
                x = x * lax.rsqrt(_mean_sq(x) + EPS) * fw_ref[...]
            o_ref[rows, :] = x
            return carry

        lax.fori_loop(0, TB // TM, finish, 0)


def _conv_glu(layer, h2, x, mod, row_len, multi_row, block_rows, row0, w, tag):
    t = h2.shape[0]
    nf = D_FF // FB
    return pl.pallas_call(
        functools.partial(_ffn_kernel, row_len, multi_row, block_rows, row0, layer == DEPTH - 1),
        grid=(t // TB, nf),
        in_specs=[
            pl.BlockSpec((TB, D_MODEL), lambda i, f: (i, 0)),
            pl.BlockSpec((TB, D_MODEL), lambda i, f: (i, 0)),
            pl.BlockSpec((1, 6, MOD_ROWS, D_MODEL), lambda i, f: (layer, 0, 0, 0)),
            pl.BlockSpec((1, D_MODEL), lambda i, f: (0, 0)),
            pl.BlockSpec((1, 1, D_MODEL, 2 * FB), lambda i, f: (layer, f, 0, 0)),
            pl.BlockSpec((1, 9, FB), lambda i, f: (layer, 0, f)),
            pl.BlockSpec((1, FB, D_MODEL), lambda i, f: (layer, f, 0)),
        ],
        out_specs=pl.BlockSpec((TB, D_MODEL), lambda i, f: (i, 0)),
        out_shape=jax.ShapeDtypeStruct((t, D_MODEL), F32),
        scratch_shapes=[
            pltpu.VMEM((TB, FB), F32),
            pltpu.VMEM((TB, FB), F32),
            pltpu.VMEM((TB, FB), BF16),
        ],
        compiler_params=_params("arbitrary", "arbitrary"),
        name=f"conv_glu_l{layer}_{tag}",
    )(h2, x, mod, w["final_norm_w"], w["w_up"], w["ffn_conv_w"], w["w_down"])


def kernel(x_prompt, x_sample, state_deltanet, c, c_ctx, w_ada, b_ada, norm1_w, w_in, conv_a_w,
           qkv_conv_w, a_log, dt_bias, o_norm_w, w_out, norm2_w, w_up, ffn_conv_w, w_down,
           final_norm_w):
    b_ctx, ctx_len, _ = x_prompt.shape
    b_lat, lat_len, _ = x_sample.shape
    assert ctx_len == TM and lat_len == TB and lat_len % GRID_W == 0 and b_lat + 1 <= MOD_ROWS
    assert (b_ctx * ctx_len) % TB == 0

    c_all = jnp.concatenate(
        [c_ctx[None, :], c, jnp.zeros((MOD_ROWS - 1 - b_lat, D_MODEL), F32)], axis=0)
    mod = _modulation(c_all, w_ada, b_ada)

    gate_rows = jnp.stack([a_log.reshape(DEPTH, -1), dt_bias.reshape(DEPTH, -1)], axis=1)
    gate_params = jnp.pad(gate_rows, ((0, 0), (0, 6), (2 * DN_HEADS, LANES - N_GATE_COLS)))
    w = {
        "norm1_w": norm1_w.reshape(DEPTH, 1, D_MODEL),
        "w_main": w_in[:, :, :MAIN_COLS].astype(BF16),
        "w_small": jnp.pad(w_in[:, :, MAIN_COLS:], ((0, 0), (0, 0), (0, LANES - N_GATE_COLS))).astype(BF16),
        "conv_a_w": conv_a_w,
        "qkv_conv_w": qkv_conv_w,
        "gate_params": gate_params,
        "o_norm_w": o_norm_w.reshape(DEPTH, 1, DV),
        "w_out": w_out.astype(BF16),
        "norm2_w": norm2_w.reshape(DEPTH, 1, D_MODEL),
        "w_up": w_up.astype(BF16).reshape(DEPTH, D_MODEL, 2, D_FF // FB, FB).transpose(0, 3, 1, 2, 4)
                    .reshape(DEPTH, D_FF // FB, D_MODEL, 2 * FB),
        "ffn_conv_w": ffn_conv_w.reshape(DEPTH, 9, D_FF),
        "w_down": w_down.astype(BF16),
        "final_norm_w": final_norm_w.reshape(1, D_MODEL),
    }

    streams = {
        "ctx": dict(x=x_prompt.reshape(b_ctx * ctx_len, D_MODEL), b=b_ctx, l=ctx_len, row_len=ctx_len,
                    tpb=0, row0=0),
        "lat": dict(x=x_sample.reshape(b_lat * lat_len, D_MODEL), b=b_lat, l=lat_len, row_len=GRID_W,
                    tpb=lat_len // TM, row0=1),
    }
    ctx_states = []
    for layer in range(DEPTH):
        for tag, s in streams.items():
            yconv, q, k, v, gate, small = _pre_mixer(layer, s["x"], mod, s["row_len"], s["tpb"],
                                                     s["row0"], w)
            shp = lambda a: a.reshape(s["b"], s["l"], a.shape[-1])
            if tag == "ctx":
                o_f, o_b, s_fin = _gdn(shp(q), shp(k), shp(v), shp(small), None, layer, True,
                                       f"gdn_l{layer}_ctx")
                ctx_states.append(s_fin)
            else:
                o_f, o_b = _gdn(shp(q), shp(k), shp(v), shp(small), state_deltanet, layer, False,
                                f"gdn_l{layer}_lat")
            flat = lambda a: a.reshape(s["b"] * s["l"], a.shape[-1])
            x_mid, h2 = _post_mixer(layer, s["x"], flat(o_f), flat(o_b), gate, yconv, mod, s["tpb"],
                                    s["row0"], w, tag)
            s["x"] = _conv_glu(layer, h2, x_mid, mod, s["row_len"], s["l"] // s["row_len"] > 1,
                               tag == "lat", s["row0"], w, tag)
    outs = {tag: s["x"].reshape(s["b"], s["l"], D_MODEL) for tag, s in streams.items()}
    new_state = jnp.stack(ctx_states, axis=1)
    return (outs["ctx"], outs["lat"], new_state)
```

```python
import functools

import jax
import jax.numpy as jnp
from jax import lax
from jax.experimental import pallas as pl
from jax.experimental.pallas import tpu as pltpu

D_MODEL = 1024
DEPTH = 2
GRID_W = 64
CONV_W = D_MODEL // 2
DN_HEADS = 4
DK = 128
DV = 128
QK_W = DN_HEADS * DK
V_W = DN_HEADS * DV
D_FF = 2816
CHUNK = 64
EPS = 1e-6
MAIN_COLS = 3 * CONV_W + 2 * QK_W + 2 * V_W
N_GATE_COLS = 4 * DN_HEADS

LANES = 128
TM = 256
TB = 2048
FB = 256
FFN_PARTS = 4
HALO = 8
CONV_ROWS = 64
INV_BASE = 16
MOD_ROWS = 16
VMEM_LIMIT = 56 * 1024 * 1024

BF16 = jnp.bfloat16
F32 = jnp.float32


def _dot(a, b):
    return jnp.dot(a, b, preferred_element_type=F32)


def _dot_nt(a, b):
    return lax.dot_general(a, b, (((1,), (1,)), ((), ())), preferred_element_type=F32)


def _dot_tn(a, b):
    return lax.dot_general(a, b, (((0,), (0,)), ((), ())), preferred_element_type=F32)


def _dot_split(a, b):
    ah = a.astype(BF16)
    al = (a - ah.astype(F32)).astype(BF16)
    bh = b.astype(BF16)
    bl = (b - bh.astype(F32)).astype(BF16)
    return _dot(ah, bh) + (_dot(ah, bl) + _dot(al, bh))


def _silu(x):
    return x * jax.nn.sigmoid(x)


def _mean_sq(x):
    return jnp.mean(x * x, axis=-1, keepdims=True)


def _params(*sem):
    return pltpu.CompilerParams(dimension_semantics=sem, vmem_limit_bytes=VMEM_LIMIT)


def _mod_kernel(c_ref, w_ref, b_ref, o_ref):
    a = _silu(c_ref[...]).astype(BF16)
    o_ref[0, 0] = _dot(a, w_ref[0].astype(BF16)) + b_ref[0]


def _modulation(c_all, w_ada, b_ada):
    return pl.pallas_call(
        _mod_kernel,
        grid=(DEPTH, 6),
        in_specs=[
            pl.BlockSpec((MOD_ROWS, D_MODEL), lambda l, n: (0, 0)),
            pl.BlockSpec((1, D_MODEL, D_MODEL), lambda l, n: (l, 0, n)),
            pl.BlockSpec((1, 1, D_MODEL), lambda l, n: (l, 0, n)),
        ],
        out_specs=pl.BlockSpec((1, 1, MOD_ROWS, D_MODEL), lambda l, n: (l, n, 0, 0)),
        out_shape=jax.ShapeDtypeStruct((DEPTH, 6, MOD_ROWS, D_MODEL), F32),
        compiler_params=_params("arbitrary", "arbitrary"),
        name="modulation",
    )(c_all, w_ada, b_ada.reshape(DEPTH, 1, 6 * D_MODEL))


def _mod_row(mod_ref, which, row):
    return mod_ref[0, which, pl.ds(row, 1), :]


def _conv3(u, w_ref, lo, first, last):
    n = u.shape[0]
    width = u.shape[1]
    up = jnp.where(first, 0.0, pltpu.roll(u, 1, 0))
    un = jnp.where(last, 0.0, pltpu.roll(u, n - 1, 0))
    w0 = w_ref[0, 0:1, lo:lo + width]
    w1 = w_ref[0, 1:2, lo:lo + width]
    w2 = w_ref[0, 2:3, lo:lo + width]
    return w0 * up + w1 * u + w2 * un


def _pre_kernel(row_len, tiles_per_batch, row0, x_ref, mod_ref, n1_ref, wm_ref, ws_ref, cw_ref,
                qw_ref, gp_ref, yc_ref, q_ref, k_ref, v_ref, g_ref, sm_ref):
    row = row0 + pl.program_id(0) // tiles_per_batch if tiles_per_batch else row0
    x = x_ref[...]
    xn = x * lax.rsqrt(_mean_sq(x) + EPS)
    h = xn * (n1_ref[0] * (1.0 + _mod_row(mod_ref, 1, row))) + _mod_row(mod_ref, 0, row)
    hb = h.astype(BF16)

    pos = lax.broadcasted_iota(jnp.int32, (TM, 1), 0) & (row_len - 1)
    first = pos == 0
    last = pos == row_len - 1

    cb = _dot(hb, wm_ref[0, :, 0:CONV_W])
    cc = _dot(hb, wm_ref[0, :, CONV_W:2 * CONV_W])
    cx = _dot(hb, wm_ref[0, :, 2 * CONV_W:3 * CONV_W])
    yc_ref[...] = (cb * _conv3(cc * cx, cw_ref, 0, first, last)).astype(BF16)

    base = 3 * CONV_W
    for idx, out_ref in enumerate((q_ref, k_ref, v_ref)):
        z = _dot(hb, wm_ref[0, :, base + idx * QK_W: base + (idx + 1) * QK_W])
        a = _silu(_conv3(z, qw_ref, idx * QK_W, first, last))
        if idx < 2:
            scale = DK ** -0.5 if idx == 0 else 1.0
            for hd in range(DN_HEADS):
                ah = a[:, hd * DK:(hd + 1) * DK]
                ss = jnp.sum(ah * ah, axis=-1, keepdims=True)
                out_ref[:, hd * DK:(hd + 1) * DK] = ah * (lax.rsqrt(ss + EPS) * scale)
        else:
            out_ref[...] = a
    g_ref[...] = _silu(_dot(hb, wm_ref[0, :, base + 3 * QK_W: base + 3 * QK_W + V_W]))

    zs = _dot(hb, ws_ref[0])
    lane = lax.broadcasted_iota(jnp.int32, (TM, LANES), 1)
    a_log = gp_ref[0, 0:1, :]
    dt_bias = gp_ref[0, 1:2, :]
    log_alpha = -jnp.exp(a_log) * jax.nn.softplus(zs + dt_bias)
    sm_ref[...] = jnp.where(lane < 2 * DN_HEADS, jax.nn.sigmoid(zs),
                            jnp.where(lane < N_GATE_COLS, log_alpha, 0.0))


def _pre_mixer(layer, x, mod, row_len, tiles_per_batch, row0, w):
    t = x.shape[0]
    tile = lambda i: (i, 0)
    lay3 = lambda i: (layer, 0, 0)
    in_specs = [
        pl.BlockSpec((TM, D_MODEL), tile),
        pl.BlockSpec((1, 6, MOD_ROWS, D_MODEL), lambda i: (layer, 0, 0, 0)),
        pl.BlockSpec((1, 1, D_MODEL), lay3),
        pl.BlockSpec((1, D_MODEL, MAIN_COLS), lay3),
        pl.BlockSpec((1, D_MODEL, LANES), lay3),
        pl.BlockSpec((1, 3, CONV_W), lay3),
        pl.BlockSpec((1, 3, 2 * QK_W + V_W), lay3),
        pl.BlockSpec((1, 8, LANES), lay3),
    ]
    args = [x, mod, w["norm1_w"], w["w_main"], w["w_small"], w["conv_a_w"], w["qkv_conv_w"],
            w["gate_params"]]
    out_specs = [pl.BlockSpec((TM, CONV_W), tile)]
    out_shape = [jax.ShapeDtypeStruct((t, CONV_W), BF16)]
    for _ in range(4):
        out_specs.append(pl.BlockSpec((TM, QK_W), tile))
        out_shape.append(jax.ShapeDtypeStruct((t, QK_W), F32))
    out_specs.append(pl.BlockSpec((TM, LANES), tile))
    out_shape.append(jax.ShapeDtypeStruct((t, LANES), F32))
    return pl.pallas_call(
        functools.partial(_pre_kernel, row_len, tiles_per_batch, row0),
        grid=(t // TM,),
        in_specs=in_specs,
        out_specs=out_specs,
        out_shape=out_shape,
        compiler_params=_params("arbitrary"),
        name=f"pre_mixer_l{layer}_r{row_len}",
    )(*args)


def _gdn_kernel(has_s0, want_state, nblk, *refs):
    refs = list(refs)
    fwd_in = refs[0:4]
    bwd_in = refs[4:8]
    pos = 8
    s0_ref = None
    if has_s0:
        s0_ref = refs[pos]
        pos += 1
    of_ref, ob_ref = refs[pos], refs[pos + 1]
    pos += 2
    so_ref = None
    if want_state:
        so_ref = refs[pos]
        pos += 1
    s_scr, gf_scr, gb_scr = refs[pos:pos + 3]

    j = pl.program_id(1)

    @pl.when(j == 0)
    def _():
        if has_s0:
            s_scr[...] = s0_ref[0, 0]
        else:
            s_scr[...] = jnp.zeros_like(s_scr)

    pc = lax.broadcasted_iota(jnp.int32, (TM, 1), 0) & (CHUNK - 1)
    g = fwd_in[3][0]
    for s in (1, 2, 4, 8, 16, 32):
        g = g + jnp.where(pc >= s, pltpu.roll(g, s, 0), 0.0)
    gf_scr[...] = g
    g = bwd_in[3][0]
    for s in (1, 2, 4, 8, 16, 32):
        g = g + jnp.where(pc < CHUNK - s, pltpu.roll(g, TM - s, 0), 0.0)
    gb_scr[...] = g

    ii = lax.broadcasted_iota(jnp.int32, (2 * CHUNK, 2 * CHUNK), 0)
    jj = lax.broadcasted_iota(jnp.int32, (2 * CHUNK, 2 * CHUNK), 1)
    blk = lambda n: (ii // n) == (jj // n)
    same = blk(CHUNK)
    eye = (ii == jj).astype(F32)
    masks = (
        ((same & (ii >= jj)).astype(F32), (same & (ii > jj)).astype(F32)),
        ((same & (ii <= jj)).astype(F32), (same & (ii < jj)).astype(F32)),
    )
    blk_diag = blk(INV_BASE).astype(F32)
    merge_masks = []
    n = INV_BASE
    while n < CHUNK:
        merge_masks.append((blk(2 * n) & jnp.logical_not(blk(n))).astype(F32))
        n *= 2

    def colb(arr, lane):
        return jnp.broadcast_to(arr[:, lane:lane + 1], (CHUNK, LANES))

    def stack2(a, b):
        return jnp.concatenate([a, b], axis=0)

    def mm(a, b):
        return _dot(a.astype(BF16), b.astype(BF16))

    n_chunks = TM // CHUNK
    n_pairs = DN_HEADS // 2
    hs = lambda hd: slice(hd * DK, (hd + 1) * DK)
    chains = [(d, c, p) for d in (0, 1) for c in range(n_chunks) for p in range(n_pairs)]
    st = {}
    for key in chains:
        d, c, p = key
        q_ref, k_ref, v_ref, sm_ref = fwd_in if d == 0 else bwd_in
        g_scr = gf_scr if d == 0 else gb_scr
        rows = slice(c * CHUNK, (c + 1) * CHUNK)
        h0, h1 = 2 * p, 2 * p + 1
        gc = g_scr[rows, :]
        sm = sm_ref[0, rows, :]
        g2 = stack2(colb(gc, 2 * DN_HEADS + DN_HEADS * d + h0),
                    colb(gc, 2 * DN_HEADS + DN_HEADS * d + h1))
        b2 = stack2(colb(sm, DN_HEADS * d + h0), colb(sm, DN_HEADS * d + h1))
        q2 = stack2(q_ref[0, rows, hs(h0)], q_ref[0, rows, hs(h1)])
        k2 = stack2(k_ref[0, rows, hs(h0)], k_ref[0, rows, hs(h1)])
        v2 = stack2(v_ref[0, rows, hs(h0)], v_ref[0, rows, hs(h1)])
        m_incl, m_strict = masks[d]
        e = jnp.exp((g2 - g2.T) * m_incl)
        eg = jnp.exp(g2)
        end_row = CHUNK - 1 if d == 0 else 0
        g_end = stack2(jnp.broadcast_to(g2[end_row:end_row + 1, :], (CHUNK, LANES)),
                       jnp.broadcast_to(g2[CHUNK + end_row:CHUNK + end_row + 1, :], (CHUNK, LANES)))
        st[key] = dict(
            k2b=k2.astype(BF16), q2b=q2.astype(BF16), b2=b2,
            dec_strict=e * m_strict, dec_incl=e * m_incl,
            rhs=jnp.concatenate([v2 * b2, k2 * (b2 * eg)], axis=1).astype(BF16),
            qd=q2 * eg, k_end=(k2 * jnp.exp(g_end - g2)).astype(BF16),
            decay=[jnp.exp(g2[hh * CHUNK + end_row: hh * CHUNK + end_row + 1, :]) for hh in (0, 1)],
        )
    for key in chains:
        c_ = st[key]
        c_["a"] = _dot_nt(c_["k2b"], c_["k2b"]) * c_["b2"] * c_["dec_strict"]
        c_["qk"] = (_dot_nt(c_["q2b"], c_["k2b"]) * c_["dec_incl"]).astype(BF16)
    for key in chains:
        c_ = st[key]
        c_["pw"] = c_["a"] * blk_diag
        c_["t"] = eye - c_["pw"]
    lvl = 2
    while lvl < INV_BASE:
        for key in chains:
            st[key]["pw"] = mm(st[key]["pw"], st[key]["pw"])
        for key in chains:
            st[key]["t"] = st[key]["t"] + mm(st[key]["t"], st[key]["pw"])
        lvl *= 2
    for m_off in merge_masks:
        for key in chains:
            st[key]["pw"] = mm(st[key]["a"] * m_off, st[key]["t"])
        for key in chains:
            st[key]["t"] = st[key]["t"] - mm(st[key]["t"], st[key]["pw"])
    for key in chains:
        st[key]["uw"] = _dot(st[key]["t"].astype(BF16), st[key]["rhs"])

    state = {(d, hd): s_scr[d, hd] for d in (0, 1) for hd in range(DN_HEADS)}
    for step in range(n_chunks):
        keys = [(d, step if d == 0 else n_chunks - 1 - step, p) for d in (0, 1) for p in range(n_pairs)]
        ws = {}
        for key in keys:
            d, c, p = key
            c_ = st[key]
            for hh in (0, 1):
                sl = slice(hh * CHUNK, (hh + 1) * CHUNK)
                lhs = stack2(c_["uw"][sl, DV:], c_["qd"][sl]).astype(BF16)
                ws[key, hh] = _dot(lhs, state[d, 2 * p + hh].astype(BF16))
        vn = {}
        for key in keys:
            d, c, p = key
            c_ = st[key]
            vn[key] = stack2(*[c_["uw"][hh * CHUNK:(hh + 1) * CHUNK, :DV] - ws[key, hh][:CHUNK]
                               for hh in (0, 1)]).astype(BF16)
            o2 = stack2(ws[key, 0][CHUNK:], ws[key, 1][CHUNK:]) + _dot(c_["qk"], vn[key])
            o_ref = of_ref if d == 0 else ob_ref
            for hh in (0, 1):
                o_ref[0, c * CHUNK:(c + 1) * CHUNK, hs(2 * p + hh)] = o2[hh * CHUNK:(hh + 1) * CHUNK]
        for key in keys:
            d, c, p = key
            c_ = st[key]
            for hh in (0, 1):
                sl = slice(hh * CHUNK, (hh + 1) * CHUNK)
                state[d, 2 * p + hh] = (state[d, 2 * p + hh] * c_["decay"][hh]
                                        + _dot_tn(c_["k_end"][sl], vn[key][sl]))
    for (d, hd), val in state.items():
        s_scr[d, hd] = val

    if want_state:
        @pl.when(j == nblk - 1)
        def _():
            so_ref[0] = s_scr[...]


def _gdn(q, k, v, small, s0, layer, want_state, name):
    b, l, _ = q.shape
    nblk = l // TM
    fwd = lambda bi, j: (bi, j, 0)
    bwd = lambda bi, j: (bi, nblk - 1 - j, 0)
    st = lambda bi, j: (bi, 0, 0, 0, 0)
    in_specs, args = [], []
    for imap in (fwd, bwd):
        in_specs += [pl.BlockSpec((1, TM, QK_W), imap)] * 3 + [pl.BlockSpec((1, TM, LANES), imap)]
        args += [q, k, v, small]
    if s0 is not None:
        in_specs.append(pl.BlockSpec((1, 1, 2, DN_HEADS, DK, DV), lambda bi, j: (bi, layer, 0, 0, 0, 0)))
        args.append(s0)
    out_specs = [pl.BlockSpec((1, TM, V_W), fwd), pl.BlockSpec((1, TM, V_W), bwd)]
    out_shape = [jax.ShapeDtypeStruct((b, l, V_W), F32)] * 2
    if want_state:
        out_specs.append(pl.BlockSpec((1, 2, DN_HEADS, DK, DV), st))
        out_shape.append(jax.ShapeDtypeStruct((b, 2, DN_HEADS, DK, DV), F32))
    return pl.pallas_call(
        functools.partial(_gdn_kernel, s0 is not None, want_state, nblk),
        grid=(b, nblk),
        in_specs=in_specs,
        out_specs=out_specs,
        out_shape=out_shape,
        scratch_shapes=[
            pltpu.VMEM((2, DN_HEADS, DK, DV), F32),
            pltpu.VMEM((TM, LANES), F32),
            pltpu.VMEM((TM, LANES), F32),
        ],
        compiler_params=_params("arbitrary", "arbitrary"),
        name=name,
    )(*args)


def _post_kernel(tiles_per_batch, row0, x_ref, of_ref, ob_ref, g_ref, yc_ref, mod_ref, onw_ref,
                 wo_ref, n2_ref, xo_ref, h2_ref):
    row = row0 + pl.program_id(0) // tiles_per_batch if tiles_per_batch else row0
    o = of_ref[...] + ob_ref[...]
    gated = []
    for hd in range(DN_HEADS):
        oh = o[:, hd * DV:(hd + 1) * DV]
        on = oh * lax.rsqrt(_mean_sq(oh) + EPS) * onw_ref[0]
        gated.append((on * g_ref[:, hd * DV:(hd + 1) * DV]).astype(BF16))
    og = jnp.concatenate(gated, axis=1)
    y = _dot(yc_ref[...], wo_ref[0, 0:CONV_W, :]) + _dot(og, wo_ref[0, CONV_W:, :])
    x = x_ref[...] + _mod_row(mod_ref, 2, row) * y
    xo_ref[...] = x
    xn = x * lax.rsqrt(_mean_sq(x) + EPS)
    h2 = xn * (n2_ref[0] * (1.0 + _mod_row(mod_ref, 4, row))) + _mod_row(mod_ref, 3, row)
    h2_ref[...] = h2.astype(BF16)


def _post_mixer(layer, x, o_f, o_b, gate, yconv, mod, tiles_per_batch, row0, w, tag):
    t = x.shape[0]
    tile = lambda i: (i, 0)
    lay3 = lambda i: (layer, 0, 0)
    return pl.pallas_call(
        functools.partial(_post_kernel, tiles_per_batch, row0),
        grid=(t // TM,),
        in_specs=[
            pl.BlockSpec((TM, D_MODEL), tile),
            pl.BlockSpec((TM, V_W), tile),
            pl.BlockSpec((TM, V_W), tile),
            pl.BlockSpec((TM, V_W), tile),
            pl.BlockSpec((TM, CONV_W), tile),
            pl.BlockSpec((1, 6, MOD_ROWS, D_MODEL), lambda i: (layer, 0, 0, 0)),
            pl.BlockSpec((1, 1, DV), lay3),
            pl.BlockSpec((1, D_MODEL, D_MODEL), lay3),
            pl.BlockSpec((1, 1, D_MODEL), lay3),
        ],
        out_specs=[pl.BlockSpec((TM, D_MODEL), tile), pl.BlockSpec((TM, D_MODEL), tile)],
        out_shape=[jax.ShapeDtypeStruct((t, D_MODEL), F32), jax.ShapeDtypeStruct((t, D_MODEL), BF16)],
        compiler_params=_params("arbitrary"),
        name=f"post_mixer_l{layer}_{tag}",
    )(x, o_f, o_b, gate, yconv, mod, w["o_norm_w"], w["w_out"], w["norm2_w"])


def _ffn_kernel(row_len, multi_row, block_rows, row0, final, h_ref, x_ref, mod_ref, fw_ref, wu_ref,
                wg_ref, cw_ref, wd_ref, o_ref, u_scr, g_scr, a_scr):
    part = TB // FFN_PARTS
    n_sub = part // CONV_ROWS
    n_slab = FB // LANES

    @pl.when(pl.program_id(1) == 0)
    def _():
        o_ref[...] = jnp.zeros_like(o_ref)
        for s in range(n_slab):
            g_scr[s, 0:HALO, :] = jnp.zeros((HALO, LANES), F32)
            g_scr[s, HALO + TB:2 * HALO + TB, :] = jnp.zeros((HALO, LANES), F32)

    def up_proj(p):
        rows = slice(p * part, (p + 1) * part)
        hb = h_ref[rows, :]
        u = _dot(hb, wu_ref[0])
        g = _dot(hb, wg_ref[0])
        for s in range(n_slab):
            u_scr[s, rows, :] = u[:, s * LANES:(s + 1) * LANES]
            g_scr[s, HALO + p * part:HALO + (p + 1) * part, :] = g[:, s * LANES:(s + 1) * LANES]

    tok = lax.broadcasted_iota(jnp.int32, (CONV_ROWS, 1), 0)

    def taps(r0, s):
        base = r0 + HALO
        g = g_scr[s, base:base + CONV_ROWS, :]
        gm = g_scr[s, base - 1:base - 1 + CONV_ROWS, :]
        gp = g_scr[s, base + 1:base + 1 + CONV_ROWS, :]
        if r0 % row_len == 0:
            gm = jnp.where(tok == 0, 0.0, gm)
        if (r0 + CONV_ROWS) % row_len == 0:
            gp = jnp.where(tok == CONV_ROWS - 1, 0.0, gp)
        return gm, g, gp

    def conv_gate(p):
        for s in range(n_slab):
            lanes = slice(s * LANES, (s + 1) * LANES)
            cw = [cw_ref[0, i:i + 1, lanes] for i in range(9)]
            cache = {}
            for sb in range(p * n_sub, (p + 1) * n_sub):
                r0 = sb * CONV_ROWS
                conv = None
                for ky in (0, 1, 2) if multi_row else (1,):
                    src = r0 + (ky - 1) * row_len
                    if src < 0 or src >= TB:
                        continue
                    if src not in cache:
                        cache[src] = taps(src, s)
                    gm, g, gp = cache[src]
                    term = cw[3 * ky] * gm + cw[3 * ky + 1] * g + cw[3 * ky + 2] * gp
                    conv = term if conv is None else conv + term
                act = jax.nn.gelu(conv) * u_scr[s, r0:r0 + CONV_ROWS, :]
                a_scr[s, r0:r0 + CONV_ROWS, :] = act.astype(BF16)

    def down_proj(p):
        rows = slice(p * part, (p + 1) * part)
        act = jnp.concatenate([a_scr[s, rows, :] for s in range(n_slab)], axis=1)
        o_ref[rows, :] += _dot(act, wd_ref[0])

    up_proj(0)
    for p in range(FFN_PARTS):
        if p + 1 < FFN_PARTS:
            up_proj(p + 1)
        if p > 0:
            down_proj(p - 1)
        conv_gate(p)
    down_proj(FFN_PARTS - 1)

    @pl.when(pl.program_id(1) == pl.num_programs(1) - 1)
    def _():
        row = row0 + pl.program_id(0) if block_rows else row0
        gate2 = _mod_row(mod_ref, 5, row)

        def finish(c, carry):
            rows = pl.ds(pl.multiple_of(c * TM, TM), TM)
            x = x_ref[rows, :] + gate2 * o_ref[rows, :]
            if final:
                x = x * lax.rsqrt(_mean_sq(x) + EPS) * fw_ref[...]
            o_ref[rows, :] = x
            return carry

        lax.fori_loop(0, TB // TM, finish, 0)


def _conv_glu(layer, h2, x, mod, row_len, multi_row, block_rows, row0, w, tag):
    t = h2.shape[0]
    nf = D_FF // FB
    return pl.pallas_call(
        functools.partial(_ffn_kernel, row_len, multi_row, block_rows, row0, layer == DEPTH - 1),
        grid=(t // TB, nf),
        in_specs=[
            pl.BlockSpec((TB, D_MODEL), lambda i, f: (i, 0)),
            pl.BlockSpec((TB, D_MODEL), lambda i, f: (i, 0)),
            pl.BlockSpec((1, 6, MOD_ROWS, D_MODEL), lambda i, f: (layer, 0, 0, 0)),
            pl.BlockSpec((1, D_MODEL), lambda i, f: (0, 0)),
            pl.BlockSpec((1, D_MODEL, FB), lambda i, f: (layer, 0, f)),
            pl.BlockSpec((1, D_MODEL, FB), lambda i, f: (layer, 0, nf + f)),
            pl.BlockSpec((1, 9, FB), lambda i, f: (layer, 0, f)),
            pl.BlockSpec((1, FB, D_MODEL), lambda i, f: (layer, f, 0)),
        ],
        out_specs=pl.BlockSpec((TB, D_MODEL), lambda i, f: (i, 0)),
        out_shape=jax.ShapeDtypeStruct((t, D_MODEL), F32),
        scratch_shapes=[
            pltpu.VMEM((FB // LANES, TB, LANES), F32),
            pltpu.VMEM((FB // LANES, TB + 2 * HALO, LANES), F32),
            pltpu.VMEM((FB // LANES, TB, LANES), BF16),
        ],
        compiler_params=_params("arbitrary", "arbitrary"),
        name=f"conv_glu_l{layer}_{tag}",
    )(h2, x, mod, w["final_norm_w"], w["w_up"], w["w_up"], w["ffn_conv_w"], w["w_down"])


def kernel(x_prompt, x_sample, state_deltanet, c, c_ctx, w_ada, b_ada, norm1_w, w_in, conv_a_w,
           qkv_conv_w, a_log, dt_bias, o_norm_w, w_out, norm2_w, w_up, ffn_conv_w, w_down,
           final_norm_w):
    b_ctx, ctx_len, _ = x_prompt.shape
    b_lat, lat_len, _ = x_sample.shape
    assert ctx_len == TM and lat_len == TB and lat_len % GRID_W == 0 and b_lat + 1 <= MOD_ROWS
    assert (b_ctx * ctx_len) % TB == 0

    c_all = jnp.concatenate(
        [c_ctx[None, :], c, jnp.zeros((MOD_ROWS - 1 - b_lat, D_MODEL), F32)], axis=0)
    mod = _modulation(c_all, w_ada, b_ada)

    gate_rows = jnp.stack([a_log.reshape(DEPTH, -1), dt_bias.reshape(DEPTH, -1)], axis=1)
    gate_params = jnp.pad(gate_rows, ((0, 0), (0, 6), (2 * DN_HEADS, LANES - N_GATE_COLS)))
    w = {
        "norm1_w": norm1_w.reshape(DEPTH, 1, D_MODEL),
        "w_main": w_in[:, :, :MAIN_COLS].astype(BF16),
        "w_small": jnp.pad(w_in[:, :, MAIN_COLS:], ((0, 0), (0, 0), (0, LANES - N_GATE_COLS))).astype(BF16),
        "conv_a_w": conv_a_w,
        "qkv_conv_w": qkv_conv_w,
        "gate_params": gate_params,
        "o_norm_w": o_norm_w.reshape(DEPTH, 1, DV),
        "w_out": w_out.astype(BF16),
        "norm2_w": norm2_w.reshape(DEPTH, 1, D_MODEL),
        "w_up": w_up.astype(BF16),
        "ffn_conv_w": ffn_conv_w.reshape(DEPTH, 9, D_FF),
        "w_down": w_down.astype(BF16),
        "final_norm_w": final_norm_w.reshape(1, D_MODEL),
    }

    streams = {
        "ctx": dict(x=x_prompt.reshape(b_ctx * ctx_len, D_MODEL), b=b_ctx, l=ctx_len, row_len=ctx_len,
                    tpb=0, row0=0),
        "lat": dict(x=x_sample.reshape(b_lat * lat_len, D_MODEL), b=b_lat, l=lat_len, row_len=GRID_W,
                    tpb=lat_len // TM, row0=1),
    }
    ctx_states = []
    for layer in range(DEPTH):
        for tag, s in streams.items():
            yconv, q, k, v, gate, small = _pre_mixer(layer, s["x"], mod, s["row_len"], s["tpb"],
                                                     s["row0"], w)
            shp = lambda a: a.reshape(s["b"], s["l"], a.shape[-1])
            if tag == "ctx":
                o_f, o_b, s_fin = _gdn(shp(q), shp(k), shp(v), shp(small), None, layer, True,
                                       f"gdn_l{layer}_ctx")
                ctx_states.append(s_fin)
            else:
                o_f, o_b = _gdn(shp(q), shp(k), shp(v), shp(small), state_deltanet, layer, False,
                                f"gdn_l{layer}_lat")
            flat = lambda a: a.reshape(s["b"] * s["l"], a.shape[-1])
            x_mid, h2 = _post_mixer(layer, s["x"], flat(o_f), flat(o_b), gate, yconv, mod, s["tpb"],
                                    s["row0"], w, tag)
            s["x"] = _conv_glu(layer, h2, x_mid, mod, s["row_len"], s["l"] // s["row_len"] > 1,
                               tag == "lat", s["row0"], w, tag)
    outs = {tag: s["x"].reshape(s["b"], s["l"], D_MODEL) for tag, s in streams.items()}
    new_state = jnp.stack(ctx_states, axis=1)
    return (outs["ctx"], outs["lat"], new_state)
```

```python
import functools

import jax
import jax.numpy as jnp
from jax import lax
from jax.experimental import pallas as pl
from jax.experimental.pallas import tpu as pltpu

D_MODEL = 1024
DEPTH = 2
GRID_W = 64
CONV_W = D_MODEL // 2
DN_HEADS = 4
DK = 128
DV = 128
QK_W = DN_HEADS * DK
V_W = DN_HEADS * DV
D_FF = 2816
CHUNK = 64
EPS = 1e-6
MAIN_COLS = 3 * CONV_W + 2 * QK_W + 2 * V_W
N_GATE_COLS = 4 * DN_HEADS

LANES = 128
TM = 256
TM_PRE = 512
TM_POST = 1024
TB = 2048
FB = 256
FFN_PARTS = 4
HALO = 8
CONV_ROWS = 64
INV_BASE = 16
MOD_ROWS = 16
VMEM_LIMIT = 56 * 1024 * 1024

BF16 = jnp.bfloat16
F32 = jnp.float32


def _dot(a, b):
    return jnp.dot(a, b, preferred_element_type=F32)


def _dot_nt(a, b):
    return lax.dot_general(a, b, (((1,), (1,)), ((), ())), preferred_element_type=F32)


def _dot_tn(a, b):
    return lax.dot_general(a, b, (((0,), (0,)), ((), ())), preferred_element_type=F32)


def _dot_split(a, b):
    ah = a.astype(BF16)
    al = (a - ah.astype(F32)).astype(BF16)
    bh = b.astype(BF16)
    bl = (b - bh.astype(F32)).astype(BF16)
    return _dot(ah, bh) + (_dot(ah, bl) + _dot(al, bh))


def _silu(x):
    return x * jax.nn.sigmoid(x)


def _gelu_tanh_bf16(x):
    c = (2.0 / jnp.pi) ** 0.5
    t = ((x * x).astype(F32) * (c * 0.044715) + c).astype(BF16)
    return (x * 0.5) * (1.0 + jnp.tanh(x * t))


def _mean_sq(x):
    return jnp.mean(x * x, axis=-1, keepdims=True)


def _params(*sem):
    return pltpu.CompilerParams(dimension_semantics=sem, vmem_limit_bytes=VMEM_LIMIT)


def _mod_kernel(c_ref, w_ref, b_ref, o_ref):
    a = _silu(c_ref[...]).astype(BF16)
    o_ref[0, 0] = _dot(a, w_ref[0].astype(BF16)) + b_ref[0]


def _modulation(c_all, w_ada, b_ada):
    return pl.pallas_call(
        _mod_kernel,
        grid=(DEPTH, 6),
        in_specs=[
            pl.BlockSpec((MOD_ROWS, D_MODEL), lambda l, n: (0, 0)),
            pl.BlockSpec((1, D_MODEL, D_MODEL), lambda l, n: (l, 0, n)),
            pl.BlockSpec((1, 1, D_MODEL), lambda l, n: (l, 0, n)),
        ],
        out_specs=pl.BlockSpec((1, 1, MOD_ROWS, D_MODEL), lambda l, n: (l, n, 0, 0)),
        out_shape=jax.ShapeDtypeStruct((DEPTH, 6, MOD_ROWS, D_MODEL), F32),
        compiler_params=_params("arbitrary", "arbitrary"),
        name="modulation",
    )(c_all, w_ada, b_ada.reshape(DEPTH, 1, 6 * D_MODEL))


def _mod_row(mod_ref, which, row):
    return mod_ref[0, which, pl.ds(row, 1), :]


def _conv3(u, w_ref, lo, first, last, scr):
    n = u.shape[0]
    width = u.shape[1]
    scr[HALO:HALO + n, :] = u
    up = jnp.where(first, 0.0, scr[HALO - 1:HALO - 1 + n, :])
    un = jnp.where(last, 0.0, scr[HALO + 1:HALO + 1 + n, :])
    w0 = w_ref[0, 0:1, lo:lo + width]
    w1 = w_ref[0, 1:2, lo:lo + width]
    w2 = w_ref[0, 2:3, lo:lo + width]
    return w0 * up + w1 * u + w2 * un


def _pre_kernel(row_len, tiles_per_batch, row0, x_ref, mod_ref, n1_ref, wm_ref, ws_ref, cw_ref,
                qw_ref, gp_ref, yc_ref, q_ref, k_ref, v_ref, g_ref, sm_ref, conv_scr):
    row = row0 + pl.program_id(0) // tiles_per_batch if tiles_per_batch else row0

    @pl.when(pl.program_id(0) == 0)
    def _():
        for i in range(conv_scr.shape[0]):
            conv_scr[i, 0:HALO, :] = jnp.zeros((HALO, CONV_W), F32)
            conv_scr[i, HALO + TM_PRE:2 * HALO + TM_PRE, :] = jnp.zeros((HALO, CONV_W), F32)

    x = x_ref[...]
    xn = x * lax.rsqrt(_mean_sq(x) + EPS)
    h = xn * (n1_ref[0] * (1.0 + _mod_row(mod_ref, 1, row))) + _mod_row(mod_ref, 0, row)
    hb = h.astype(BF16)

    pos = lax.broadcasted_iota(jnp.int32, (TM_PRE, 1), 0) & (row_len - 1)
    first = pos == 0
    last = pos == row_len - 1

    cb = _dot(hb, wm_ref[0, :, 0:CONV_W])
    cc = _dot(hb, wm_ref[0, :, CONV_W:2 * CONV_W])
    cx = _dot(hb, wm_ref[0, :, 2 * CONV_W:3 * CONV_W])
    yc_ref[...] = (cb * _conv3(cc * cx, cw_ref, 0, first, last, conv_scr.at[0])).astype(BF16)

    base = 3 * CONV_W
    for idx, out_ref in enumerate((q_ref, k_ref, v_ref)):
        z = _dot(hb, wm_ref[0, :, base + idx * QK_W: base + (idx + 1) * QK_W])
        a = _silu(_conv3(z, qw_ref, idx * QK_W, first, last, conv_scr.at[idx + 1]))
        if idx < 2:
            scale = DK ** -0.5 if idx == 0 else 1.0
            for hd in range(DN_HEADS):
                ah = a[:, hd * DK:(hd + 1) * DK]
                ss = jnp.sum(ah * ah, axis=-1, keepdims=True)
                out_ref[:, hd * DK:(hd + 1) * DK] = ah * (lax.rsqrt(ss + EPS) * scale)
        else:
            out_ref[...] = a
    g_ref[...] = _silu(_dot(hb, wm_ref[0, :, base + 3 * QK_W: base + 3 * QK_W + V_W]))

    zs = _dot(hb, ws_ref[0])
    lane = lax.broadcasted_iota(jnp.int32, (TM_PRE, LANES), 1)
    a_log = gp_ref[0, 0:1, :]
    dt_bias = gp_ref[0, 1:2, :]
    log_alpha = -jnp.exp(a_log) * jax.nn.softplus(zs + dt_bias)
    sm_ref[...] = jnp.where(lane < 2 * DN_HEADS, jax.nn.sigmoid(zs),
                            jnp.where(lane < N_GATE_COLS, log_alpha, 0.0))


def _pre_mixer(layer, x, mod, row_len, tiles_per_batch, row0, w):
    t = x.shape[0]
    tile = lambda i: (i, 0)
    lay3 = lambda i: (layer, 0, 0)
    in_specs = [
        pl.BlockSpec((TM_PRE, D_MODEL), tile),
        pl.BlockSpec((1, 6, MOD_ROWS, D_MODEL), lambda i: (layer, 0, 0, 0)),
        pl.BlockSpec((1, 1, D_MODEL), lay3),
        pl.BlockSpec((1, D_MODEL, MAIN_COLS), lay3),
        pl.BlockSpec((1, D_MODEL, LANES), lay3),
        pl.BlockSpec((1, 3, CONV_W), lay3),
        pl.BlockSpec((1, 3, 2 * QK_W + V_W), lay3),
        pl.BlockSpec((1, 8, LANES), lay3),
    ]
    args = [x, mod, w["norm1_w"], w["w_main"], w["w_small"], w["conv_a_w"], w["qkv_conv_w"],
            w["gate_params"]]
    out_specs = [pl.BlockSpec((TM_PRE, CONV_W), tile)]
    out_shape = [jax.ShapeDtypeStruct((t, CONV_W), BF16)]
    for _ in range(4):
        out_specs.append(pl.BlockSpec((TM_PRE, QK_W), tile))
        out_shape.append(jax.ShapeDtypeStruct((t, QK_W), F32))
    out_specs.append(pl.BlockSpec((TM_PRE, LANES), tile))
    out_shape.append(jax.ShapeDtypeStruct((t, LANES), F32))
    return pl.pallas_call(
        functools.partial(_pre_kernel, row_len, tiles_per_batch, row0),
        grid=(t // TM_PRE,),
        in_specs=in_specs,
        out_specs=out_specs,
        out_shape=out_shape,
        scratch_shapes=[pltpu.VMEM((4, TM_PRE + 2 * HALO, CONV_W), F32)],
        compiler_params=_params("arbitrary"),
        name=f"pre_mixer_l{layer}_r{row_len}",
    )(*args)


def _gdn_kernel(has_s0, want_state, nblk, *refs):
    refs = list(refs)
    fwd_in = refs[0:4]
    bwd_in = refs[4:8]
    pos = 8
    s0_ref = None
    if has_s0:
        s0_ref = refs[pos]
        pos += 1
    of_ref, ob_ref = refs[pos], refs[pos + 1]
    pos += 2
    so_ref = None
    if want_state:
        so_ref = refs[pos]
        pos += 1
    s_scr, gf_scr, gb_scr = refs[pos:pos + 3]

    j = pl.program_id(1)

    @pl.when(j == 0)
    def _():
        if has_s0:
            s_scr[...] = s0_ref[0, 0]
        else:
            s_scr[...] = jnp.zeros_like(s_scr)

    pc = lax.broadcasted_iota(jnp.int32, (TM, 1), 0) & (CHUNK - 1)
    g = fwd_in[3][0]
    for s in (1, 2, 4, 8, 16, 32):
        g = g + jnp.where(pc >= s, pltpu.roll(g, s, 0), 0.0)
    gf_scr[...] = g
    g = bwd_in[3][0]
    for s in (1, 2, 4, 8, 16, 32):
        g = g + jnp.where(pc < CHUNK - s, pltpu.roll(g, TM - s, 0), 0.0)
    gb_scr[...] = g

    ii = lax.broadcasted_iota(jnp.int32, (2 * CHUNK, 2 * CHUNK), 0)
    jj = lax.broadcasted_iota(jnp.int32, (2 * CHUNK, 2 * CHUNK), 1)
    blk = lambda n: (ii // n) == (jj // n)
    same = blk(CHUNK)
    eye = (ii == jj).astype(F32)
    masks = (
        ((same & (ii >= jj)).astype(F32), (same & (ii > jj)).astype(F32)),
        ((same & (ii <= jj)).astype(F32), (same & (ii < jj)).astype(F32)),
    )
    blk_diag = blk(INV_BASE).astype(F32)
    merge_masks = []
    n = INV_BASE
    while n < CHUNK:
        merge_masks.append((blk(2 * n) & jnp.logical_not(blk(n))).astype(F32))
        n *= 2

    def colb(arr, lane):
        return jnp.broadcast_to(arr[:, lane:lane + 1], (CHUNK, LANES))

    def stack2(a, b):
        return jnp.concatenate([a, b], axis=0)

    def mm(a, b):
        return _dot(a.astype(BF16), b.astype(BF16))

    n_chunks = TM // CHUNK
    n_pairs = DN_HEADS // 2
    hs = lambda hd: slice(hd * DK, (hd + 1) * DK)
    chains = [(d, c, p) for d in (0, 1) for c in range(n_chunks) for p in range(n_pairs)]
    st = {}
    for key in chains:
        d, c, p = key
        q_ref, k_ref, v_ref, sm_ref = fwd_in if d == 0 else bwd_in
        g_scr = gf_scr if d == 0 else gb_scr
        rows = slice(c * CHUNK, (c + 1) * CHUNK)
        h0, h1 = 2 * p, 2 * p + 1
        gc = g_scr[rows, :]
        sm = sm_ref[0, rows, :]
        g2 = stack2(colb(gc, 2 * DN_HEADS + DN_HEADS * d + h0),
                    colb(gc, 2 * DN_HEADS + DN_HEADS * d + h1))
        b2 = stack2(colb(sm, DN_HEADS * d + h0), colb(sm, DN_HEADS * d + h1))
        q2 = stack2(q_ref[0, rows, hs(h0)], q_ref[0, rows, hs(h1)])
        k2 = stack2(k_ref[0, rows, hs(h0)], k_ref[0, rows, hs(h1)])
        v2 = stack2(v_ref[0, rows, hs(h0)], v_ref[0, rows, hs(h1)])
        m_incl, m_strict = masks[d]
        e = jnp.exp((g2 - g2.T) * m_incl)
        eg = jnp.exp(g2)
        end_row = CHUNK - 1 if d == 0 else 0
        g_end = stack2(jnp.broadcast_to(g2[end_row:end_row + 1, :], (CHUNK, LANES)),
                       jnp.broadcast_to(g2[CHUNK + end_row:CHUNK + end_row + 1, :], (CHUNK, LANES)))
        st[key] = dict(
            k2b=k2.astype(BF16), q2b=q2.astype(BF16), b2=b2,
            dec_strict=e * m_strict, dec_incl=e * m_incl,
            rhs=jnp.concatenate([v2 * b2, k2 * (b2 * eg)], axis=1).astype(BF16),
            qd=q2 * eg, k_end=(k2 * jnp.exp(g_end - g2)).astype(BF16),
            decay=[jnp.exp(g2[hh * CHUNK + end_row: hh * CHUNK + end_row + 1, :]) for hh in (0, 1)],
        )
    for key in chains:
        c_ = st[key]
        c_["a"] = _dot_nt(c_["k2b"], c_["k2b"]) * c_["b2"] * c_["dec_strict"]
        c_["qk"] = (_dot_nt(c_["q2b"], c_["k2b"]) * c_["dec_incl"]).astype(BF16)
    for key in chains:
        c_ = st[key]
        c_["pw"] = c_["a"] * blk_diag
        c_["t"] = eye - c_["pw"]
    lvl = 2
    while lvl < INV_BASE:
        for key in chains:
            st[key]["pw"] = mm(st[key]["pw"], st[key]["pw"])
        for key in chains:
            st[key]["t"] = st[key]["t"] + mm(st[key]["t"], st[key]["pw"])
        lvl *= 2
    for m_off in merge_masks:
        for key in chains:
            st[key]["pw"] = mm(st[key]["a"] * m_off, st[key]["t"])
        for key in chains:
            st[key]["t"] = st[key]["t"] - mm(st[key]["t"], st[key]["pw"])
    for key in chains:
        st[key]["uw"] = _dot(st[key]["t"].astype(BF16), st[key]["rhs"])

    state = {(d, hd): s_scr[d, hd] for d in (0, 1) for hd in range(DN_HEADS)}
    for step in range(n_chunks):
        keys = [(d, step if d == 0 else n_chunks - 1 - step, p) for d in (0, 1) for p in range(n_pairs)]
        ws = {}
        for key in keys:
            d, c, p = key
            c_ = st[key]
            for hh in (0, 1):
                sl = slice(hh * CHUNK, (hh + 1) * CHUNK)
                lhs = stack2(c_["uw"][sl, DV:], c_["qd"][sl]).astype(BF16)
                ws[key, hh] = _dot(lhs, state[d, 2 * p + hh].astype(BF16))
        vn = {}
        for key in keys:
            d, c, p = key
            c_ = st[key]
            vn[key] = stack2(*[c_["uw"][hh * CHUNK:(hh + 1) * CHUNK, :DV] - ws[key, hh][:CHUNK]
                               for hh in (0, 1)]).astype(BF16)
            o2 = stack2(ws[key, 0][CHUNK:], ws[key, 1][CHUNK:]) + _dot(c_["qk"], vn[key])
            o_ref = of_ref if d == 0 else ob_ref
            for hh in (0, 1):
                o_ref[0, c * CHUNK:(c + 1) * CHUNK, hs(2 * p + hh)] = o2[hh * CHUNK:(hh + 1) * CHUNK]
        for key in keys:
            d, c, p = key
            c_ = st[key]
            for hh in (0, 1):
                sl = slice(hh * CHUNK, (hh + 1) * CHUNK)
                state[d, 2 * p + hh] = (state[d, 2 * p + hh] * c_["decay"][hh]
                                        + _dot_tn(c_["k_end"][sl], vn[key][sl]))
    for (d, hd), val in state.items():
        s_scr[d, hd] = val

    if want_state:
        @pl.when(j == nblk - 1)
        def _():
            so_ref[0] = s_scr[...]


def _gdn(q, k, v, small, s0, layer, want_state, name):
    b, l, _ = q.shape
    nblk = l // TM
    fwd = lambda bi, j: (bi, j, 0)
    bwd = lambda bi, j: (bi, nblk - 1 - j, 0)
    st = lambda bi, j: (bi, 0, 0, 0, 0)
    in_specs, args = [], []
    for imap in (fwd, bwd):
        in_specs += [pl.BlockSpec((1, TM, QK_W), imap)] * 3 + [pl.BlockSpec((1, TM, LANES), imap)]
        args += [q, k, v, small]
    if s0 is not None:
        in_specs.append(pl.BlockSpec((1, 1, 2, DN_HEADS, DK, DV), lambda bi, j: (bi, layer, 0, 0, 0, 0)))
        args.append(s0)
    out_specs = [pl.BlockSpec((1, TM, V_W), fwd), pl.BlockSpec((1, TM, V_W), bwd)]
    out_shape = [jax.ShapeDtypeStruct((b, l, V_W), F32)] * 2
    if want_state:
        out_specs.append(pl.BlockSpec((1, 2, DN_HEADS, DK, DV), st))
        out_shape.append(jax.ShapeDtypeStruct((b, 2, DN_HEADS, DK, DV), F32))
    return pl.pallas_call(
        functools.partial(_gdn_kernel, s0 is not None, want_state, nblk),
        grid=(b, nblk),
        in_specs=in_specs,
        out_specs=out_specs,
        out_shape=out_shape,
        scratch_shapes=[
            pltpu.VMEM((2, DN_HEADS, DK, DV), F32),
            pltpu.VMEM((TM, LANES), F32),
            pltpu.VMEM((TM, LANES), F32),
        ],
        compiler_params=_params("arbitrary", "arbitrary"),
        name=name,
    )(*args)


def _post_kernel(tiles_per_batch, row0, x_ref, of_ref, ob_ref, g_ref, yc_ref, mod_ref, onw_ref,
                 wo_ref, n2_ref, xo_ref, h2_ref):
    row = row0 + pl.program_id(0) // tiles_per_batch if tiles_per_batch else row0
    o = of_ref[...] + ob_ref[...]
    gated = []
    for hd in range(DN_HEADS):
        oh = o[:, hd * DV:(hd + 1) * DV]
        on = oh * lax.rsqrt(_mean_sq(oh) + EPS) * onw_ref[0]
        gated.append((on * g_ref[:, hd * DV:(hd + 1) * DV]).astype(BF16))
    og = jnp.concatenate(gated, axis=1)
    y = _dot(yc_ref[...], wo_ref[0, 0:CONV_W, :]) + _dot(og, wo_ref[0, CONV_W:, :])
    x = x_ref[...] + _mod_row(mod_ref, 2, row) * y
    xo_ref[...] = x
    xn = x * lax.rsqrt(_mean_sq(x) + EPS)
    h2 = xn * (n2_ref[0] * (1.0 + _mod_row(mod_ref, 4, row))) + _mod_row(mod_ref, 3, row)
    h2_ref[...] = h2.astype(BF16)


def _post_mixer(layer, x, o_f, o_b, gate, yconv, mod, tiles_per_batch, row0, w, tag):
    t = x.shape[0]
    tile = lambda i: (i, 0)
    lay3 = lambda i: (layer, 0, 0)
    return pl.pallas_call(
        functools.partial(_post_kernel, tiles_per_batch, row0),
        grid=(t // TM_POST,),
        in_specs=[
            pl.BlockSpec((TM_POST, D_MODEL), tile),
            pl.BlockSpec((TM_POST, V_W), tile),
            pl.BlockSpec((TM_POST, V_W), tile),
            pl.BlockSpec((TM_POST, V_W), tile),
            pl.BlockSpec((TM_POST, CONV_W), tile),
            pl.BlockSpec((1, 6, MOD_ROWS, D_MODEL), lambda i: (layer, 0, 0, 0)),
            pl.BlockSpec((1, 1, DV), lay3),
            pl.BlockSpec((1, D_MODEL, D_MODEL), lay3),
            pl.BlockSpec((1, 1, D_MODEL), lay3),
        ],
        out_specs=[pl.BlockSpec((TM_POST, D_MODEL), tile), pl.BlockSpec((TM_POST, D_MODEL), tile)],
        out_shape=[jax.ShapeDtypeStruct((t, D_MODEL), F32), jax.ShapeDtypeStruct((t, D_MODEL), BF16)],
        compiler_params=_params("arbitrary"),
        name=f"post_mixer_l{layer}_{tag}",
    )(x, o_f, o_b, gate, yconv, mod, w["o_norm_w"], w["w_out"], w["norm2_w"])


def _ffn_kernel(row_len, multi_row, block_rows, row0, final, h_ref, x_ref, mod_ref, fw_ref, wu_ref,
                wg_ref, cw_ref, wd_ref, o_ref, u_scr, g_scr, a_scr):
    part = TB // FFN_PARTS
    n_sub = part // CONV_ROWS
    n_slab = FB // LANES

    @pl.when(pl.program_id(1) == 0)
    def _():
        o_ref[...] = jnp.zeros_like(o_ref)
        for s in range(n_slab):
            g_scr[s, 0:HALO, :] = jnp.zeros((HALO, LANES), F32)
            g_scr[s, HALO + TB:2 * HALO + TB, :] = jnp.zeros((HALO, LANES), F32)

    def up_proj(p):
        rows = slice(p * part, (p + 1) * part)
        hb = h_ref[rows, :]
        u = _dot(hb, wu_ref[0])
        g = _dot(hb, wg_ref[0])
        for s in range(n_slab):
            u_scr[s, rows, :] = u[:, s * LANES:(s + 1) * LANES]
            g_scr[s, HALO + p * part:HALO + (p + 1) * part, :] = g[:, s * LANES:(s + 1) * LANES]

    tok = lax.broadcasted_iota(jnp.int32, (CONV_ROWS, 1), 0)

    def taps(r0, s):
        base = r0 + HALO
        g = g_scr[s, base:base + CONV_ROWS, :]
        gm = g_scr[s, base - 1:base - 1 + CONV_ROWS, :]
        gp = g_scr[s, base + 1:base + 1 + CONV_ROWS, :]
        if r0 % row_len == 0:
            gm = jnp.where(tok == 0, 0.0, gm)
        if (r0 + CONV_ROWS) % row_len == 0:
            gp = jnp.where(tok == CONV_ROWS - 1, 0.0, gp)
        return gm.astype(BF16), g.astype(BF16), gp.astype(BF16)

    def conv_gate(p):
        for s in range(n_slab):
            lanes = slice(s * LANES, (s + 1) * LANES)
            cw = [cw_ref[0, i:i + 1, lanes].astype(BF16) for i in range(9)]
            cache = {}
            for sb in range(p * n_sub, (p + 1) * n_sub):
                r0 = sb * CONV_ROWS
                conv = None
                for ky in (0, 1, 2) if multi_row else (1,):
                    src = r0 + (ky - 1) * row_len
                    if src < 0 or src >= TB:
                        continue
                    if src not in cache:
                        cache[src] = taps(src, s)
                    gm, g, gp = cache[src]
                    term = cw[3 * ky] * gm + cw[3 * ky + 1] * g + cw[3 * ky + 2] * gp
                    conv = term if conv is None else conv + term
                act = _gelu_tanh_bf16(conv) * u_scr[s, r0:r0 + CONV_ROWS, :].astype(BF16)
                a_scr[s, r0:r0 + CONV_ROWS, :] = act

    def down_proj(p):
        rows = slice(p * part, (p + 1) * part)
        act = jnp.concatenate([a_scr[s, rows, :] for s in range(n_slab)], axis=1)
        o_ref[rows, :] += _dot(act, wd_ref[0])

    up_proj(0)
    for p in range(FFN_PARTS):
        if p + 1 < FFN_PARTS:
            up_proj(p + 1)
        if p > 0:
            down_proj(p - 1)
        conv_gate(p)
    down_proj(FFN_PARTS - 1)

    @pl.when(pl.program_id(1) == pl.num_programs(1) - 1)
    def _():
        row = row0 + pl.program_id(0) if block_rows else row0
        gate2 = _mod_row(mod_ref, 5, row)

        def finish(c, carry):
            rows = pl.ds(pl.multiple_of(c * TM, TM), TM)
            x = x_ref[rows, :] + gate2 * o_ref[rows, :]
            if final:
                x = x * lax.rsqrt(_mean_sq(x) + EPS) * fw_ref[...]
            o_ref[rows, :] = x
            return carry

        lax.fori_loop(0, TB // TM, finish, 0)


def _conv_glu(layer, h2, x, mod, row_len, multi_row, block_rows, row0, w, tag):
    t = h2.shape[0]
    nf = D_FF // FB
    return pl.pallas_call(
        functools.partial(_ffn_kernel, row_len, multi_row, block_rows, row0, layer == DEPTH - 1),
        grid=(t // TB, nf),
        in_specs=[
            pl.BlockSpec((TB, D_MODEL), lambda i, f: (i, 0)),
            pl.BlockSpec((TB, D_MODEL), lambda i, f: (i, 0)),
            pl.BlockSpec((1, 6, MOD_ROWS, D_MODEL), lambda i, f: (layer, 0, 0, 0)),
            pl.BlockSpec((1, D_MODEL), lambda i, f: (0, 0)),
            pl.BlockSpec((1, D_MODEL, FB), lambda i, f: (layer, 0, f)),
            pl.BlockSpec((1, D_MODEL, FB), lambda i, f: (layer, 0, nf + f)),
            pl.BlockSpec((1, 9, FB), lambda i, f: (layer, 0, f)),
            pl.BlockSpec((1, FB, D_MODEL), lambda i, f: (layer, f, 0)),
        ],
        out_specs=pl.BlockSpec((TB, D_MODEL), lambda i, f: (i, 0)),
        out_shape=jax.ShapeDtypeStruct((t, D_MODEL), F32),
        scratch_shapes=[
            pltpu.VMEM((FB // LANES, TB, LANES), F32),
            pltpu.VMEM((FB // LANES, TB + 2 * HALO, LANES), F32),
            pltpu.VMEM((FB // LANES, TB, LANES), BF16),
        ],
        compiler_params=_params("arbitrary", "arbitrary"),
        name=f"conv_glu_l{layer}_{tag}",
    )(h2, x, mod, w["final_norm_w"], w["w_up"], w["w_up"], w["ffn_conv_w"], w["w_down"])


def kernel(x_prompt, x_sample, state_deltanet, c, c_ctx, w_ada, b_ada, norm1_w, w_in, conv_a_w,
           qkv_conv_w, a_log, dt_bias, o_norm_w, w_out, norm2_w, w_up, ffn_conv_w, w_down,
           final_norm_w):
    b_ctx, ctx_len, _ = x_prompt.shape
    b_lat, lat_len, _ = x_sample.shape
    assert ctx_len == TM and lat_len == TB and lat_len % GRID_W == 0 and b_lat + 1 <= MOD_ROWS
    assert (b_ctx * ctx_len) % TB == 0

    c_all = jnp.concatenate(
        [c_ctx[None, :], c, jnp.zeros((MOD_ROWS - 1 - b_lat, D_MODEL), F32)], axis=0)
    mod = _modulation(c_all, w_ada, b_ada)

    gate_rows = jnp.stack([a_log.reshape(DEPTH, -1), dt_bias.reshape(DEPTH, -1)], axis=1)
    gate_params = jnp.pad(gate_rows, ((0, 0), (0, 6), (2 * DN_HEADS, LANES - N_GATE_COLS)))
    w = {
        "norm1_w": norm1_w.reshape(DEPTH, 1, D_MODEL),
        "w_main": w_in[:, :, :MAIN_COLS].astype(BF16),
        "w_small": jnp.pad(w_in[:, :, MAIN_COLS:], ((0, 0), (0, 0), (0, LANES - N_GATE_COLS))).astype(BF16),
        "conv_a_w": conv_a_w,
        "qkv_conv_w": qkv_conv_w,
        "gate_params": gate_params,
        "o_norm_w": o_norm_w.reshape(DEPTH, 1, DV),
        "w_out": w_out.astype(BF16),
        "norm2_w": norm2_w.reshape(DEPTH, 1, D_MODEL),
        "w_up": w_up.astype(BF16),
        "ffn_conv_w": ffn_conv_w.reshape(DEPTH, 9, D_FF),
        "w_down": w_down.astype(BF16),
        "final_norm_w": final_norm_w.reshape(1, D_MODEL),
    }

    streams = {
        "ctx": dict(x=x_prompt.reshape(b_ctx * ctx_len, D_MODEL), b=b_ctx, l=ctx_len, row_len=ctx_len,
                    row0=0),
        "lat": dict(x=x_sample.reshape(b_lat * lat_len, D_MODEL), b=b_lat, l=lat_len, row_len=GRID_W,
                    row0=1),
    }
    ctx_states = []
    for layer in range(DEPTH):
        for tag, s in streams.items():
            tpb = lambda tile: s["l"] // tile if tag == "lat" else 0
            yconv, q, k, v, gate, small = _pre_mixer(layer, s["x"], mod, s["row_len"], tpb(TM_PRE),
                                                     s["row0"], w)
            shp = lambda a: a.reshape(s["b"], s["l"], a.shape[-1])
            if tag == "ctx":
                o_f, o_b, s_fin = _gdn(shp(q), shp(k), shp(v), shp(small), None, layer, True,
                                       f"gdn_l{layer}_ctx")
                ctx_states.append(s_fin)
            else:
                o_f, o_b = _gdn(shp(q), shp(k), shp(v), shp(small), state_deltanet, layer, False,
                                f"gdn_l{layer}_lat")
            flat = lambda a: a.reshape(s["b"] * s["l"], a.shape[-1])
            x_mid, h2 = _post_mixer(layer, s["x"], flat(o_f), flat(o_b), gate, yconv, mod,
                                    tpb(TM_POST), s["row0"], w, tag)
            s["x"] = _conv_glu(layer, h2, x_mid, mod, s["row_len"], s["l"] // s["row_len"] > 1,
                               tag == "lat", s["row0"], w, tag)
    outs = {tag: s["x"].reshape(s["b"], s["l"], D_MODEL) for tag, s in streams.items()}
    new_state = jnp.stack(ctx_states, axis=1)
    return (outs["ctx"], outs["lat"], new_state)
```

```python
import functools

import jax
import jax.numpy as jnp
from jax import lax
from jax.experimental import pallas as pl
from jax.experimental.pallas import tpu as pltpu

D_MODEL = 1024
DEPTH = 2
GRID_W = 64
CONV_W = D_MODEL // 2
DN_HEADS = 4
DK = 128
DV = 128
QK_W = DN_HEADS * DK
V_W = DN_HEADS * DV
D_FF = 2816
CHUNK = 64
EPS = 1e-6
MAIN_COLS = 3 * CONV_W + 2 * QK_W + 2 * V_W
N_GATE_COLS = 4 * DN_HEADS

LANES = 128
TM = 256
TM_PRE = 512
TM_POST = 1024
TB = 2048
FB = 256
FFN_PARTS = 2
HALO = 8
CONV_ROWS = 64
INV_BASE = 16
MOD_ROWS = 16
VMEM_LIMIT = 56 * 1024 * 1024

BF16 = jnp.bfloat16
F32 = jnp.float32


def _dot(a, b):
    return jnp.dot(a, b, preferred_element_type=F32)


def _dot_nt(a, b):
    return lax.dot_general(a, b, (((1,), (1,)), ((), ())), preferred_element_type=F32)


def _dot_tn(a, b):
    return lax.dot_general(a, b, (((0,), (0,)), ((), ())), preferred_element_type=F32)


def _dot_split(a, b):
    ah = a.astype(BF16)
    al = (a - ah.astype(F32)).astype(BF16)
    bh = b.astype(BF16)
    bl = (b - bh.astype(F32)).astype(BF16)
    return _dot(ah, bh) + (_dot(ah, bl) + _dot(al, bh))


def _silu(x):
    return x * jax.nn.sigmoid(x)


def _gelu_tanh_bf16(x):
    c = (2.0 / jnp.pi) ** 0.5
    t = ((x * x).astype(F32) * (c * 0.044715) + c).astype(BF16)
    return (x * 0.5) * (1.0 + jnp.tanh(x * t))


def _mean_sq(x):
    return jnp.mean(x * x, axis=-1, keepdims=True)


def _params(*sem):
    return pltpu.CompilerParams(dimension_semantics=sem, vmem_limit_bytes=VMEM_LIMIT)


def _mod_kernel(c_ref, w_ref, b_ref, o_ref):
    a = _silu(c_ref[...]).astype(BF16)
    o_ref[0, 0] = _dot(a, w_ref[0].astype(BF16)) + b_ref[0]


def _modulation(c_all, w_ada, b_ada):
    return pl.pallas_call(
        _mod_kernel,
        grid=(DEPTH, 6),
        in_specs=[
            pl.BlockSpec((MOD_ROWS, D_MODEL), lambda l, n: (0, 0)),
            pl.BlockSpec((1, D_MODEL, D_MODEL), lambda l, n: (l, 0, n)),
            pl.BlockSpec((1, 1, D_MODEL), lambda l, n: (l, 0, n)),
        ],
        out_specs=pl.BlockSpec((1, 1, MOD_ROWS, D_MODEL), lambda l, n: (l, n, 0, 0)),
        out_shape=jax.ShapeDtypeStruct((DEPTH, 6, MOD_ROWS, D_MODEL), F32),
        compiler_params=_params("arbitrary", "arbitrary"),
        name="modulation",
    )(c_all, w_ada, b_ada.reshape(DEPTH, 1, 6 * D_MODEL))


def _mod_row(mod_ref, which, row):
    return mod_ref[0, which, pl.ds(row, 1), :]


def _conv3(u, w_ref, lo, first, last, scr):
    n = u.shape[0]
    width = u.shape[1]
    scr[HALO:HALO + n, :] = u
    up = jnp.where(first, 0.0, scr[HALO - 1:HALO - 1 + n, :])
    un = jnp.where(last, 0.0, scr[HALO + 1:HALO + 1 + n, :])
    w0 = w_ref[0, 0:1, lo:lo + width]
    w1 = w_ref[0, 1:2, lo:lo + width]
    w2 = w_ref[0, 2:3, lo:lo + width]
    return w0 * up + w1 * u + w2 * un


def _pre_kernel(row_len, tiles_per_batch, row0, x_ref, mod_ref, n1_ref, wm_ref, ws_ref, cw_ref,
                qw_ref, gp_ref, yc_ref, q_ref, k_ref, v_ref, g_ref, sm_ref, conv_scr):
    row = row0 + pl.program_id(0) // tiles_per_batch if tiles_per_batch else row0

    @pl.when(pl.program_id(0) == 0)
    def _():
        for i in range(conv_scr.shape[0]):
            conv_scr[i, 0:HALO, :] = jnp.zeros((HALO, CONV_W), F32)
            conv_scr[i, HALO + TM_PRE:2 * HALO + TM_PRE, :] = jnp.zeros((HALO, CONV_W), F32)

    x = x_ref[...]
    xn = x * lax.rsqrt(_mean_sq(x) + EPS)
    h = xn * (n1_ref[0] * (1.0 + _mod_row(mod_ref, 1, row))) + _mod_row(mod_ref, 0, row)
    hb = h.astype(BF16)

    pos = lax.broadcasted_iota(jnp.int32, (TM_PRE, 1), 0) & (row_len - 1)
    first = pos == 0
    last = pos == row_len - 1

    cb = _dot(hb, wm_ref[0, :, 0:CONV_W])
    cc = _dot(hb, wm_ref[0, :, CONV_W:2 * CONV_W])
    cx = _dot(hb, wm_ref[0, :, 2 * CONV_W:3 * CONV_W])
    yc_ref[...] = (cb * _conv3(cc * cx, cw_ref, 0, first, last, conv_scr.at[0])).astype(BF16)

    base = 3 * CONV_W
    for idx, out_ref in enumerate((q_ref, k_ref, v_ref)):
        z = _dot(hb, wm_ref[0, :, base + idx * QK_W: base + (idx + 1) * QK_W])
        a = _silu(_conv3(z, qw_ref, idx * QK_W, first, last, conv_scr.at[idx + 1]))
        if idx < 2:
            scale = DK ** -0.5 if idx == 0 else 1.0
            for hd in range(DN_HEADS):
                ah = a[:, hd * DK:(hd + 1) * DK]
                ss = jnp.sum(ah * ah, axis=-1, keepdims=True)
                out_ref[:, hd * DK:(hd + 1) * DK] = ah * (lax.rsqrt(ss + EPS) * scale)
        else:
            out_ref[...] = a
    g_ref[...] = _silu(_dot(hb, wm_ref[0, :, base + 3 * QK_W: base + 3 * QK_W + V_W]))

    zs = _dot(hb, ws_ref[0])
    lane = lax.broadcasted_iota(jnp.int32, (TM_PRE, LANES), 1)
    a_log = gp_ref[0, 0:1, :]
    dt_bias = gp_ref[0, 1:2, :]
    log_alpha = -jnp.exp(a_log) * jax.nn.softplus(zs + dt_bias)
    sm_ref[...] = jnp.where(lane < 2 * DN_HEADS, jax.nn.sigmoid(zs),
                            jnp.where(lane < N_GATE_COLS, log_alpha, 0.0))


def _pre_mixer(layer, x, mod, row_len, tiles_per_batch, row0, w):
    t = x.shape[0]
    tile = lambda i: (i, 0)
    lay3 = lambda i: (layer, 0, 0)
    in_specs = [
        pl.BlockSpec((TM_PRE, D_MODEL), tile),
        pl.BlockSpec((1, 6, MOD_ROWS, D_MODEL), lambda i: (layer, 0, 0, 0)),
        pl.BlockSpec((1, 1, D_MODEL), lay3),
        pl.BlockSpec((1, D_MODEL, MAIN_COLS), lay3),
        pl.BlockSpec((1, D_MODEL, LANES), lay3),
        pl.BlockSpec((1, 3, CONV_W), lay3),
        pl.BlockSpec((1, 3, 2 * QK_W + V_W), lay3),
        pl.BlockSpec((1, 8, LANES), lay3),
    ]
    args = [x, mod, w["norm1_w"], w["w_main"], w["w_small"], w["conv_a_w"], w["qkv_conv_w"],
            w["gate_params"]]
    out_specs = [pl.BlockSpec((TM_PRE, CONV_W), tile)]
    out_shape = [jax.ShapeDtypeStruct((t, CONV_W), BF16)]
    for _ in range(4):
        out_specs.append(pl.BlockSpec((TM_PRE, QK_W), tile))
        out_shape.append(jax.ShapeDtypeStruct((t, QK_W), F32))
    out_specs.append(pl.BlockSpec((TM_PRE, LANES), tile))
    out_shape.append(jax.ShapeDtypeStruct((t, LANES), F32))
    return pl.pallas_call(
        functools.partial(_pre_kernel, row_len, tiles_per_batch, row0),
        grid=(t // TM_PRE,),
        in_specs=in_specs,
        out_specs=out_specs,
        out_shape=out_shape,
        scratch_shapes=[pltpu.VMEM((4, TM_PRE + 2 * HALO, CONV_W), F32)],
        compiler_params=_params("arbitrary"),
        name=f"pre_mixer_l{layer}_r{row_len}",
    )(*args)


def _gdn_kernel(has_s0, want_state, nblk, *refs):
    refs = list(refs)
    fwd_in = refs[0:4]
    bwd_in = refs[4:8]
    pos = 8
    s0_ref = None
    if has_s0:
        s0_ref = refs[pos]
        pos += 1
    of_ref, ob_ref = refs[pos], refs[pos + 1]
    pos += 2
    so_ref = None
    if want_state:
        so_ref = refs[pos]
        pos += 1
    s_scr, gf_scr, gb_scr = refs[pos:pos + 3]

    j = pl.program_id(1)

    @pl.when(j == 0)
    def _():
        if has_s0:
            s_scr[...] = s0_ref[0, 0]
        else:
            s_scr[...] = jnp.zeros_like(s_scr)

    pc = lax.broadcasted_iota(jnp.int32, (TM, 1), 0) & (CHUNK - 1)
    g = fwd_in[3][0]
    for s in (1, 2, 4, 8, 16, 32):
        g = g + jnp.where(pc >= s, pltpu.roll(g, s, 0), 0.0)
    gf_scr[...] = g
    g = bwd_in[3][0]
    for s in (1, 2, 4, 8, 16, 32):
        g = g + jnp.where(pc < CHUNK - s, pltpu.roll(g, TM - s, 0), 0.0)
    gb_scr[...] = g

    ii = lax.broadcasted_iota(jnp.int32, (2 * CHUNK, 2 * CHUNK), 0)
    jj = lax.broadcasted_iota(jnp.int32, (2 * CHUNK, 2 * CHUNK), 1)
    blk = lambda n: (ii // n) == (jj // n)
    same = blk(CHUNK)
    eye = (ii == jj).astype(F32)
    masks = (
        ((same & (ii >= jj)).astype(F32), (same & (ii > jj)).astype(F32)),
        ((same & (ii <= jj)).astype(F32), (same & (ii < jj)).astype(F32)),
    )
    blk_diag = blk(INV_BASE).astype(F32)
    merge_masks = []
    n = INV_BASE
    while n < CHUNK:
        merge_masks.append((blk(2 * n) & jnp.logical_not(blk(n))).astype(F32))
        n *= 2

    def colb(arr, lane):
        return jnp.broadcast_to(arr[:, lane:lane + 1], (CHUNK, LANES))

    def stack2(a, b):
        return jnp.concatenate([a, b], axis=0)

    def mm(a, b):
        return _dot(a.astype(BF16), b.astype(BF16))

    n_chunks = TM // CHUNK
    n_pairs = DN_HEADS // 2
    hs = lambda hd: slice(hd * DK, (hd + 1) * DK)
    chains = [(d, c, p) for d in (0, 1) for c in range(n_chunks) for p in range(n_pairs)]
    st = {}

    def operands(keys):
        for key in keys:
            d, c, p = key
            q_ref, k_ref, v_ref, sm_ref = fwd_in if d == 0 else bwd_in
            g_scr = gf_scr if d == 0 else gb_scr
            rows = slice(c * CHUNK, (c + 1) * CHUNK)
            h0, h1 = 2 * p, 2 * p + 1
            gc = g_scr[rows, :]
            sm = sm_ref[0, rows, :]
            g2 = stack2(colb(gc, 2 * DN_HEADS + DN_HEADS * d + h0),
                        colb(gc, 2 * DN_HEADS + DN_HEADS * d + h1))
            b2 = stack2(colb(sm, DN_HEADS * d + h0), colb(sm, DN_HEADS * d + h1))
            q2 = stack2(q_ref[0, rows, hs(h0)], q_ref[0, rows, hs(h1)])
            k2 = stack2(k_ref[0, rows, hs(h0)], k_ref[0, rows, hs(h1)])
            v2 = stack2(v_ref[0, rows, hs(h0)], v_ref[0, rows, hs(h1)])
            m_incl, m_strict = masks[d]
            e = jnp.exp((g2 - g2.T) * m_incl)
            eg = jnp.exp(g2)
            end_row = CHUNK - 1 if d == 0 else 0
            g_end = stack2(jnp.broadcast_to(g2[end_row:end_row + 1, :], (CHUNK, LANES)),
                           jnp.broadcast_to(g2[CHUNK + end_row:CHUNK + end_row + 1, :], (CHUNK, LANES)))
            st[key] = dict(
                k2b=k2.astype(BF16), q2b=q2.astype(BF16), b2=b2,
                dec_strict=e * m_strict, dec_incl=e * m_incl,
                rhs=jnp.concatenate([v2 * b2, k2 * (b2 * eg)], axis=1).astype(BF16),
                qd=q2 * eg, k_end=(k2 * jnp.exp(g_end - g2)).astype(BF16),
                decay=[jnp.exp(g2[hh * CHUNK + end_row: hh * CHUNK + end_row + 1, :]) for hh in (0, 1)],
            )

    def gram(keys):
        for key in keys:
            c_ = st[key]
            c_["a"] = _dot_nt(c_["k2b"], c_["k2b"]) * c_["b2"] * c_["dec_strict"]
            c_["qk"] = (_dot_nt(c_["q2b"], c_["k2b"]) * c_["dec_incl"]).astype(BF16)
            c_["pw"] = c_["a"] * blk_diag
            c_["t"] = eye - c_["pw"]

    def square(keys):
        for key in keys:
            st[key]["pw"] = mm(st[key]["pw"], st[key]["pw"])

    def accumulate(keys):
        for key in keys:
            st[key]["t"] = st[key]["t"] + mm(st[key]["t"], st[key]["pw"])

    def merge(keys):
        for key in keys:
            st[key]["t"] = st[key]["t"] - mm(st[key]["t"], st[key]["pw"])

    def solve(keys):
        for key in keys:
            st[key]["uw"] = _dot(st[key]["t"].astype(BF16), st[key]["rhs"])

    stages = [operands, gram]
    lvl = 2
    while lvl < INV_BASE:
        stages += [square, accumulate]
        lvl *= 2
    for m_off in merge_masks:
        def off_times_t(keys, m_off=m_off):
            for key in keys:
                st[key]["pw"] = mm(st[key]["a"] * m_off, st[key]["t"])

        stages += [off_times_t, merge]
    stages.append(solve)

    state = {(d, hd): s_scr[d, hd] for d in (0, 1) for hd in range(DN_HEADS)}
    ws, vn = {}, {}

    def step_keys(step):
        return [(d, step if d == 0 else n_chunks - 1 - step, p) for d in (0, 1) for p in range(n_pairs)]

    def apply_state(step):
        for key in step_keys(step):
            d, c, p = key
            c_ = st[key]
            for hh in (0, 1):
                sl = slice(hh * CHUNK, (hh + 1) * CHUNK)
                lhs = stack2(c_["uw"][sl, DV:], c_["qd"][sl]).astype(BF16)
                ws[key, hh] = _dot(lhs, state[d, 2 * p + hh].astype(BF16))

    def emit_output(step):
        for key in step_keys(step):
            d, c, p = key
            c_ = st[key]
            vn[key] = stack2(*[c_["uw"][hh * CHUNK:(hh + 1) * CHUNK, :DV] - ws[key, hh][:CHUNK]
                               for hh in (0, 1)]).astype(BF16)
            o2 = stack2(ws[key, 0][CHUNK:], ws[key, 1][CHUNK:]) + _dot(c_["qk"], vn[key])
            o_ref = of_ref if d == 0 else ob_ref
            for hh in (0, 1):
                o_ref[0, c * CHUNK:(c + 1) * CHUNK, hs(2 * p + hh)] = o2[hh * CHUNK:(hh + 1) * CHUNK]

    def update_state(step):
        for key in step_keys(step):
            d, c, p = key
            c_ = st[key]
            for hh in (0, 1):
                sl = slice(hh * CHUNK, (hh + 1) * CHUNK)
                state[d, 2 * p + hh] = (state[d, 2 * p + hh] * c_["decay"][hh]
                                        + _dot_tn(c_["k_end"][sl], vn[key][sl]))

    def sub_steps(steps):
        return [functools.partial(f, step) for step in steps for f in (apply_state, emit_output, update_state)]

    for stage in stages:
        stage(chains)
    for sub in sub_steps(range(n_chunks)):
        sub()
    for (d, hd), val in state.items():
        s_scr[d, hd] = val

    if want_state:
        @pl.when(j == nblk - 1)
        def _():
            so_ref[0] = s_scr[...]


def _gdn(q, k, v, small, s0, layer, want_state, name):
    b, l, _ = q.shape
    nblk = l // TM
    fwd = lambda bi, j: (bi, j, 0)
    bwd = lambda bi, j: (bi, nblk - 1 - j, 0)
    st = lambda bi, j: (bi, 0, 0, 0, 0)
    in_specs, args = [], []
    for imap in (fwd, bwd):
        in_specs += [pl.BlockSpec((1, TM, QK_W), imap)] * 3 + [pl.BlockSpec((1, TM, LANES), imap)]
        args += [q, k, v, small]
    if s0 is not None:
        in_specs.append(pl.BlockSpec((1, 1, 2, DN_HEADS, DK, DV), lambda bi, j: (bi, layer, 0, 0, 0, 0)))
        args.append(s0)
    out_specs = [pl.BlockSpec((1, TM, V_W), fwd), pl.BlockSpec((1, TM, V_W), bwd)]
    out_shape = [jax.ShapeDtypeStruct((b, l, V_W), F32)] * 2
    if want_state:
        out_specs.append(pl.BlockSpec((1, 2, DN_HEADS, DK, DV), st))
        out_shape.append(jax.ShapeDtypeStruct((b, 2, DN_HEADS, DK, DV), F32))
    return pl.pallas_call(
        functools.partial(_gdn_kernel, s0 is not None, want_state, nblk),
        grid=(b, nblk),
        in_specs=in_specs,
        out_specs=out_specs,
        out_shape=out_shape,
        scratch_shapes=[
            pltpu.VMEM((2, DN_HEADS, DK, DV), F32),
            pltpu.VMEM((TM, LANES), F32),
            pltpu.VMEM((TM, LANES), F32),
        ],
        compiler_params=_params("arbitrary", "arbitrary"),
        name=name,
    )(*args)


def _post_kernel(tiles_per_batch, row0, x_ref, of_ref, ob_ref, g_ref, yc_ref, mod_ref, onw_ref,
                 wo_ref, n2_ref, xo_ref, h2_ref):
    row = row0 + pl.program_id(0) // tiles_per_batch if tiles_per_batch else row0
    o = of_ref[...] + ob_ref[...]
    gated = []
    for hd in range(DN_HEADS):
        oh = o[:, hd * DV:(hd + 1) * DV]
        on = oh * lax.rsqrt(_mean_sq(oh) + EPS) * onw_ref[0]
        gated.append((on * g_ref[:, hd * DV:(hd + 1) * DV]).astype(BF16))
    og = jnp.concatenate(gated, axis=1)
    y = _dot(yc_ref[...], wo_ref[0, 0:CONV_W, :]) + _dot(og, wo_ref[0, CONV_W:, :])
    x = x_ref[...] + _mod_row(mod_ref, 2, row) * y
    xo_ref[...] = x
    xn = x * lax.rsqrt(_mean_sq(x) + EPS)
    h2 = xn * (n2_ref[0] * (1.0 + _mod_row(mod_ref, 4, row))) + _mod_row(mod_ref, 3, row)
    h2_ref[...] = h2.astype(BF16)


def _post_mixer(layer, x, o_f, o_b, gate, yconv, mod, tiles_per_batch, row0, w, tag):
    t = x.shape[0]
    tile = lambda i: (i, 0)
    lay3 = lambda i: (layer, 0, 0)
    return pl.pallas_call(
        functools.partial(_post_kernel, tiles_per_batch, row0),
        grid=(t // TM_POST,),
        in_specs=[
            pl.BlockSpec((TM_POST, D_MODEL), tile),
            pl.BlockSpec((TM_POST, V_W), tile),
            pl.BlockSpec((TM_POST, V_W), tile),
            pl.BlockSpec((TM_POST, V_W), tile),
            pl.BlockSpec((TM_POST, CONV_W), tile),
            pl.BlockSpec((1, 6, MOD_ROWS, D_MODEL), lambda i: (layer, 0, 0, 0)),
            pl.BlockSpec((1, 1, DV), lay3),
            pl.BlockSpec((1, D_MODEL, D_MODEL), lay3),
            pl.BlockSpec((1, 1, D_MODEL), lay3),
        ],
        out_specs=[pl.BlockSpec((TM_POST, D_MODEL), tile), pl.BlockSpec((TM_POST, D_MODEL), tile)],
        out_shape=[jax.ShapeDtypeStruct((t, D_MODEL), F32), jax.ShapeDtypeStruct((t, D_MODEL), BF16)],
        compiler_params=_params("arbitrary"),
        name=f"post_mixer_l{layer}_{tag}",
    )(x, o_f, o_b, gate, yconv, mod, w["o_norm_w"], w["w_out"], w["norm2_w"])


def _ffn_kernel(row_len, multi_row, block_rows, row0, final, h_ref, x_ref, mod_ref, fw_ref, wu_ref,
                wg_ref, cw_ref, wd_ref, o_ref, u_scr, g_scr, a_scr):
    part = TB // FFN_PARTS
    n_sub = part // CONV_ROWS
    n_slab = FB // LANES

    @pl.when(pl.program_id(1) == 0)
    def _():
        o_ref[...] = jnp.zeros_like(o_ref)
        for s in range(n_slab):
            g_scr[s, 0:HALO, :] = jnp.zeros((HALO, LANES), F32)
            g_scr[s, HALO + TB:2 * HALO + TB, :] = jnp.zeros((HALO, LANES), F32)

    def up_proj(p):
        rows = slice(p * part, (p + 1) * part)
        hb = h_ref[rows, :]
        u = _dot(hb, wu_ref[0])
        g = _dot(hb, wg_ref[0])
        for s in range(n_slab):
            u_scr[s, rows, :] = u[:, s * LANES:(s + 1) * LANES].astype(BF16)
            g_scr[s, HALO + p * part:HALO + (p + 1) * part, :] = g[:, s * LANES:(s + 1) * LANES]

    tok = lax.broadcasted_iota(jnp.int32, (CONV_ROWS, 1), 0)

    def taps(r0, s):
        base = r0 + HALO
        g = g_scr[s, base:base + CONV_ROWS, :]
        gm = g_scr[s, base - 1:base - 1 + CONV_ROWS, :]
        gp = g_scr[s, base + 1:base + 1 + CONV_ROWS, :]
        if r0 % row_len == 0:
            gm = jnp.where(tok == 0, 0.0, gm)
        if (r0 + CONV_ROWS) % row_len == 0:
            gp = jnp.where(tok == CONV_ROWS - 1, 0.0, gp)
        return gm.astype(BF16), g.astype(BF16), gp.astype(BF16)

    def conv_gate(p):
        for s in range(n_slab):
            lanes = slice(s * LANES, (s + 1) * LANES)
            cw = [cw_ref[0, i:i + 1, lanes].astype(BF16) for i in range(9)]
            cache = {}
            for sb in range(p * n_sub, (p + 1) * n_sub):
                r0 = sb * CONV_ROWS
                conv = None
                for ky in (0, 1, 2) if multi_row else (1,):
                    src = r0 + (ky - 1) * row_len
                    if src < 0 or src >= TB:
                        continue
                    if src not in cache:
                        cache[src] = taps(src, s)
                    gm, g, gp = cache[src]
                    term = cw[3 * ky] * gm + cw[3 * ky + 1] * g + cw[3 * ky + 2] * gp
                    conv = term if conv is None else conv + term
                act = _gelu_tanh_bf16(conv) * u_scr[s, r0:r0 + CONV_ROWS, :]
                a_scr[s, r0:r0 + CONV_ROWS, :] = act

    def down_proj(p):
        rows = slice(p * part, (p + 1) * part)
        act = jnp.concatenate([a_scr[s, rows, :] for s in range(n_slab)], axis=1)
        o_ref[rows, :] += _dot(act, wd_ref[0])

    up_proj(0)
    for p in range(FFN_PARTS):
        if p + 1 < FFN_PARTS:
            up_proj(p + 1)
        if p > 0:
            down_proj(p - 1)
        conv_gate(p)
    down_proj(FFN_PARTS - 1)

    @pl.when(pl.program_id(1) == pl.num_programs(1) - 1)
    def _():
        row = row0 + pl.program_id(0) if block_rows else row0
        gate2 = _mod_row(mod_ref, 5, row)

        def finish(c, carry):
            rows = pl.ds(pl.multiple_of(c * TM, TM), TM)
            x = x_ref[rows, :] + gate2 * o_ref[rows, :]
            if final:
                x = x * lax.rsqrt(_mean_sq(x) + EPS) * fw_ref[...]
            o_ref[rows, :] = x
            return carry

        lax.fori_loop(0, TB // TM, finish, 0)


def _conv_glu(layer, h2, x, mod, row_len, multi_row, block_rows, row0, w, tag):
    t = h2.shape[0]
    nf = D_FF // FB
    return pl.pallas_call(
        functools.partial(_ffn_kernel, row_len, multi_row, block_rows, row0, layer == DEPTH - 1),
        grid=(t // TB, nf),
        in_specs=[
            pl.BlockSpec((TB, D_MODEL), lambda i, f: (i, 0)),
            pl.BlockSpec((TB, D_MODEL), lambda i, f: (i, 0)),
            pl.BlockSpec((1, 6, MOD_ROWS, D_MODEL), lambda i, f: (layer, 0, 0, 0)),
            pl.BlockSpec((1, D_MODEL), lambda i, f: (0, 0)),
            pl.BlockSpec((1, D_MODEL, FB), lambda i, f: (layer, 0, f)),
            pl.BlockSpec((1, D_MODEL, FB), lambda i, f: (layer, 0, nf + f)),
            pl.BlockSpec((1, 9, FB), lambda i, f: (layer, 0, f)),
            pl.BlockSpec((1, FB, D_MODEL), lambda i, f: (layer, f, 0)),
        ],
        out_specs=pl.BlockSpec((TB, D_MODEL), lambda i, f: (i, 0)),
        out_shape=jax.ShapeDtypeStruct((t, D_MODEL), F32),
        scratch_shapes=[
            pltpu.VMEM((FB // LANES, TB, LANES), BF16),
            pltpu.VMEM((FB // LANES, TB + 2 * HALO, LANES), F32),
            pltpu.VMEM((FB // LANES, TB, LANES), BF16),
        ],
        compiler_params=_params("arbitrary", "arbitrary"),
        name=f"conv_glu_l{layer}_{tag}",
    )(h2, x, mod, w["final_norm_w"], w["w_up"], w["w_up"], w["ffn_conv_w"], w["w_down"])


def kernel(x_prompt, x_sample, state_deltanet, c, c_ctx, w_ada, b_ada, norm1_w, w_in, conv_a_w,
           qkv_conv_w, a_log, dt_bias, o_norm_w, w_out, norm2_w, w_up, ffn_conv_w, w_down,
           final_norm_w):
    b_ctx, ctx_len, _ = x_prompt.shape
    b_lat, lat_len, _ = x_sample.shape
    assert ctx_len == TM and lat_len == TB and lat_len % GRID_W == 0 and b_lat + 1 <= MOD_ROWS
    assert (b_ctx * ctx_len) % TB == 0

    c_all = jnp.concatenate(
        [c_ctx[None, :], c, jnp.zeros((MOD_ROWS - 1 - b_lat, D_MODEL), F32)], axis=0)
    mod = _modulation(c_all, w_ada, b_ada)

    gate_rows = jnp.stack([a_log.reshape(DEPTH, -1), dt_bias.reshape(DEPTH, -1)], axis=1)
    gate_params = jnp.pad(gate_rows, ((0, 0), (0, 6), (2 * DN_HEADS, LANES - N_GATE_COLS)))
    w = {
        "norm1_w": norm1_w.reshape(DEPTH, 1, D_MODEL),
        "w_main": w_in[:, :, :MAIN_COLS].astype(BF16),
        "w_small": jnp.pad(w_in[:, :, MAIN_COLS:], ((0, 0), (0, 0), (0, LANES - N_GATE_COLS))).astype(BF16),
        "conv_a_w": conv_a_w,
        "qkv_conv_w": qkv_conv_w,
        "gate_params": gate_params,
        "o_norm_w": o_norm_w.reshape(DEPTH, 1, DV),
        "w_out": w_out.astype(BF16),
        "norm2_w": norm2_w.reshape(DEPTH, 1, D_MODEL),
        "w_up": w_up.astype(BF16),
        "ffn_conv_w": ffn_conv_w.reshape(DEPTH, 9, D_FF),
        "w_down": w_down.astype(BF16),
        "final_norm_w": final_norm_w.reshape(1, D_MODEL),
    }

    streams = {
        "ctx": dict(x=x_prompt.reshape(b_ctx * ctx_len, D_MODEL), b=b_ctx, l=ctx_len, row_len=ctx_len,
                    row0=0),
        "lat": dict(x=x_sample.reshape(b_lat * lat_len, D_MODEL), b=b_lat, l=lat_len, row_len=GRID_W,
                    row0=1),
    }
    ctx_states = []
    for layer in range(DEPTH):
        for tag, s in streams.items():
            tpb = lambda tile: s["l"] // tile if tag == "lat" else 0
            yconv, q, k, v, gate, small = _pre_mixer(layer, s["x"], mod, s["row_len"], tpb(TM_PRE),
                                                     s["row0"], w)
            shp = lambda a: a.reshape(s["b"], s["l"], a.shape[-1])
            if tag == "ctx":
                o_f, o_b, s_fin = _gdn(shp(q), shp(k), shp(v), shp(small), None, layer, True,
                                       f"gdn_l{layer}_ctx")
                ctx_states.append(s_fin)
            else:
                o_f, o_b = _gdn(shp(q), shp(k), shp(v), shp(small), state_deltanet, layer, False,
                                f"gdn_l{layer}_lat")
            flat = lambda a: a.reshape(s["b"] * s["l"], a.shape[-1])
            x_mid, h2 = _post_mixer(layer, s["x"], flat(o_f), flat(o_b), gate, yconv, mod,
                                    tpb(TM_POST), s["row0"], w, tag)
            s["x"] = _conv_glu(layer, h2, x_mid, mod, s["row_len"], s["l"] // s["row_len"] > 1,
                               tag == "lat", s["row0"], w, tag)
    outs = {tag: s["x"].reshape(s["b"], s["l"], D_MODEL) for tag, s in streams.items()}
    new_state = jnp.stack(ctx_states, axis=1)
    return (outs["ctx"], outs["lat"], new_state)
```

```python
import functools

import jax
import jax.numpy as jnp
from jax import lax
from jax.experimental import pallas as pl
from jax.experimental.pallas import tpu as pltpu

D_MODEL = 1024
DEPTH = 2
GRID_W = 64
CONV_W = D_MODEL // 2
DN_HEADS = 4
DK = 128
DV = 128
QK_W = DN_HEADS * DK
V_W = DN_HEADS * DV
D_FF = 2816
CHUNK = 64
EPS = 1e-6
MAIN_COLS = 3 * CONV_W + 2 * QK_W + 2 * V_W
N_GATE_COLS = 4 * DN_HEADS

LANES = 128
TM = 256
TM_PRE = 512
TM_POST = 1024
TB = 2048
FB = 256
FFN_PARTS = 2
HALO = 8
CONV_ROWS = 64
INV_BASE = 16
MOD_ROWS = 16
VMEM_LIMIT = 56 * 1024 * 1024

BF16 = jnp.bfloat16
F32 = jnp.float32


def _dot(a, b):
    return jnp.dot(a, b, preferred_element_type=F32)


def _dot_nt(a, b):
    return lax.dot_general(a, b, (((1,), (1,)), ((), ())), preferred_element_type=F32)


def _dot_tn(a, b):
    return lax.dot_general(a, b, (((0,), (0,)), ((), ())), preferred_element_type=F32)


def _silu(x):
    return x * jax.nn.sigmoid(x)


def _gelu_tanh_bf16(x):
    c = (2.0 / jnp.pi) ** 0.5
    t = ((x * x).astype(F32) * (c * 0.044715) + c).astype(BF16)
    return (x * 0.5) * (1.0 + jnp.tanh(x * t))


def _mean_sq(x):
    return jnp.mean(x * x, axis=-1, keepdims=True)


def _params(*sem):
    return pltpu.CompilerParams(dimension_semantics=sem, vmem_limit_bytes=VMEM_LIMIT)


def _mod_kernel(c_ref, w_ref, b_ref, o_ref):
    a = _silu(c_ref[...]).astype(BF16)
    o_ref[0, 0] = _dot(a, w_ref[0].astype(BF16)) + b_ref[0]


def _modulation(c_all, w_ada, b_ada):
    return pl.pallas_call(
        _mod_kernel,
        grid=(DEPTH, 6),
        in_specs=[
            pl.BlockSpec((MOD_ROWS, D_MODEL), lambda l, n: (0, 0)),
            pl.BlockSpec((1, D_MODEL, D_MODEL), lambda l, n: (l, 0, n)),
            pl.BlockSpec((1, 1, D_MODEL), lambda l, n: (l, 0, n)),
        ],
        out_specs=pl.BlockSpec((1, 1, MOD_ROWS, D_MODEL), lambda l, n: (l, n, 0, 0)),
        out_shape=jax.ShapeDtypeStruct((DEPTH, 6, MOD_ROWS, D_MODEL), F32),
        compiler_params=_params("arbitrary", "arbitrary"),
        name="modulation",
    )(c_all, w_ada, b_ada.reshape(DEPTH, 1, 6 * D_MODEL))


def _mod_row(mod_ref, which, row):
    return mod_ref[0, which, pl.ds(row, 1), :]


def _conv3(u, w_ref, lo, row_len, scr):
    n = u.shape[0]
    width = u.shape[1]
    pitch = row_len + HALO
    starts = [HALO + r * pitch for r in range(n // row_len)]
    for r, s0 in enumerate(starts):
        scr[s0:s0 + row_len, :] = u[r * row_len:(r + 1) * row_len]
    up = jnp.concatenate([scr[s0 - 1:s0 - 1 + row_len, :] for s0 in starts], axis=0)
    un = jnp.concatenate([scr[s0 + 1:s0 + 1 + row_len, :] for s0 in starts], axis=0)
    w0 = w_ref[0, 0:1, lo:lo + width]
    w1 = w_ref[0, 1:2, lo:lo + width]
    w2 = w_ref[0, 2:3, lo:lo + width]
    return w0 * up + w1 * u + w2 * un


def _pre_kernel(row_len, tiles_per_batch, row0, x_ref, mod_ref, n1_ref, wm_ref, ws_ref, cw_ref,
                qw_ref, gp_ref, yc_ref, q_ref, k_ref, v_ref, g_ref, sm_ref, conv_scr):
    row = row0 + pl.program_id(0) // tiles_per_batch if tiles_per_batch else row0

    @pl.when(pl.program_id(0) == 0)
    def _():
        conv_scr[...] = jnp.zeros_like(conv_scr)

    x = x_ref[...]
    xn = x * lax.rsqrt(_mean_sq(x) + EPS)
    h = xn * (n1_ref[0] * (1.0 + _mod_row(mod_ref, 1, row))) + _mod_row(mod_ref, 0, row)
    hb = h.astype(BF16)

    cb = _dot(hb, wm_ref[0, :, 0:CONV_W])
    cc = _dot(hb, wm_ref[0, :, CONV_W:2 * CONV_W])
    cx = _dot(hb, wm_ref[0, :, 2 * CONV_W:3 * CONV_W])
    yc_ref[...] = (cb * _conv3(cc * cx, cw_ref, 0, row_len, conv_scr.at[0])).astype(BF16)

    base = 3 * CONV_W
    for idx, out_ref in enumerate((q_ref, k_ref, v_ref)):
        z = _dot(hb, wm_ref[0, :, base + idx * QK_W: base + (idx + 1) * QK_W])
        a = _silu(_conv3(z, qw_ref, idx * QK_W, row_len, conv_scr.at[idx + 1]))
        if idx < 2:
            scale = DK ** -0.5 if idx == 0 else 1.0
            for hd in range(DN_HEADS):
                ah = a[:, hd * DK:(hd + 1) * DK]
                ss = jnp.sum(ah * ah, axis=-1, keepdims=True)
                out_ref[:, hd * DK:(hd + 1) * DK] = ah * (lax.rsqrt(ss + EPS) * scale)
        else:
            out_ref[...] = a
    g_ref[...] = _silu(_dot(hb, wm_ref[0, :, base + 3 * QK_W: base + 3 * QK_W + V_W])).astype(BF16)

    zs = _dot(hb, ws_ref[0])
    lane = lax.broadcasted_iota(jnp.int32, (TM_PRE, LANES), 1)
    a_log = gp_ref[0, 0:1, :]
    dt_bias = gp_ref[0, 1:2, :]
    log_alpha = -jnp.exp(a_log) * jax.nn.softplus(zs + dt_bias)
    sm_ref[...] = jnp.where(lane < 2 * DN_HEADS, jax.nn.sigmoid(zs),
                            jnp.where(lane < N_GATE_COLS, log_alpha, 0.0))


def _pre_mixer(layer, x, mod, row_len, tiles_per_batch, row0, w):
    t = x.shape[0]
    tile = lambda i: (i, 0)
    lay3 = lambda i: (layer, 0, 0)
    in_specs = [
        pl.BlockSpec((TM_PRE, D_MODEL), tile),
        pl.BlockSpec((1, 6, MOD_ROWS, D_MODEL), lambda i: (layer, 0, 0, 0)),
        pl.BlockSpec((1, 1, D_MODEL), lay3),
        pl.BlockSpec((1, D_MODEL, MAIN_COLS), lay3),
        pl.BlockSpec((1, D_MODEL, LANES), lay3),
        pl.BlockSpec((1, 3, CONV_W), lay3),
        pl.BlockSpec((1, 3, 2 * QK_W + V_W), lay3),
        pl.BlockSpec((1, 8, LANES), lay3),
    ]
    args = [x, mod, w["norm1_w"], w["w_main"], w["w_small"], w["conv_a_w"], w["qkv_conv_w"],
            w["gate_params"]]
    out_specs = [pl.BlockSpec((TM_PRE, CONV_W), tile)]
    out_shape = [jax.ShapeDtypeStruct((t, CONV_W), BF16)]
    for dtype in (F32, F32, F32, BF16):
        out_specs.append(pl.BlockSpec((TM_PRE, QK_W), tile))
        out_shape.append(jax.ShapeDtypeStruct((t, QK_W), dtype))
    out_specs.append(pl.BlockSpec((TM_PRE, LANES), tile))
    out_shape.append(jax.ShapeDtypeStruct((t, LANES), F32))
    return pl.pallas_call(
        functools.partial(_pre_kernel, row_len, tiles_per_batch, row0),
        grid=(t // TM_PRE,),
        in_specs=in_specs,
        out_specs=out_specs,
        out_shape=out_shape,
        scratch_shapes=[pltpu.VMEM((4, HALO + (TM_PRE // row_len) * (row_len + HALO), CONV_W), F32)],
        compiler_params=_params("arbitrary"),
        name=f"pre_mixer_l{layer}_r{row_len}",
    )(*args)


def _gdn_kernel(has_s0, want_state, nblk, *refs):
    refs = list(refs)
    fwd_in = refs[0:4]
    bwd_in = refs[4:8]
    pos = 8
    s0_ref = None
    if has_s0:
        s0_ref = refs[pos]
        pos += 1
    of_ref, ob_ref = refs[pos], refs[pos + 1]
    pos += 2
    so_ref = None
    if want_state:
        so_ref = refs[pos]
        pos += 1
    s_scr, gf_scr, gb_scr = refs[pos:pos + 3]

    j = pl.program_id(1)

    @pl.when(j == 0)
    def _():
        if has_s0:
            s_scr[...] = s0_ref[0, 0]
        else:
            s_scr[...] = jnp.zeros_like(s_scr)

    pc = lax.broadcasted_iota(jnp.int32, (TM, 1), 0) & (CHUNK - 1)
    g = fwd_in[3][0]
    for s in (1, 2, 4, 8, 16, 32):
        g = g + jnp.where(pc >= s, pltpu.roll(g, s, 0), 0.0)
    gf_scr[...] = g
    g = bwd_in[3][0]
    for s in (1, 2, 4, 8, 16, 32):
        g = g + jnp.where(pc < CHUNK - s, pltpu.roll(g, TM - s, 0), 0.0)
    gb_scr[...] = g

    ii = lax.broadcasted_iota(jnp.int32, (2 * CHUNK, 2 * CHUNK), 0)
    jj = lax.broadcasted_iota(jnp.int32, (2 * CHUNK, 2 * CHUNK), 1)
    blk = lambda n: (ii // n) == (jj // n)
    same = blk(CHUNK)
    eye = (ii == jj).astype(F32)
    masks = (
        ((same & (ii >= jj)).astype(F32), (same & (ii > jj)).astype(F32)),
        ((same & (ii <= jj)).astype(F32), (same & (ii < jj)).astype(F32)),
    )
    blk_diag = blk(INV_BASE).astype(F32)
    merge_masks = []
    n = INV_BASE
    while n < CHUNK:
        merge_masks.append((blk(2 * n) & jnp.logical_not(blk(n))).astype(F32))
        n *= 2

    def colb(arr, lane):
        return jnp.broadcast_to(arr[:, lane:lane + 1], (CHUNK, LANES))

    def stack2(a, b):
        return jnp.concatenate([a, b], axis=0)

    n_chunks = TM // CHUNK
    n_pairs = DN_HEADS // 2
    hs = lambda hd: slice(hd * DK, (hd + 1) * DK)
    chains = [(d, c, p) for d in (0, 1) for c in range(n_chunks) for p in range(n_pairs)]
    st = {}

    def operands(keys):
        for key in keys:
            d, c, p = key
            q_ref, k_ref, v_ref, sm_ref = fwd_in if d == 0 else bwd_in
            g_scr = gf_scr if d == 0 else gb_scr
            rows = slice(c * CHUNK, (c + 1) * CHUNK)
            h0, h1 = 2 * p, 2 * p + 1
            gc = g_scr[rows, :]
            sm = sm_ref[0, rows, :]
            g2 = stack2(colb(gc, 2 * DN_HEADS + DN_HEADS * d + h0),
                        colb(gc, 2 * DN_HEADS + DN_HEADS * d + h1))
            b2 = stack2(colb(sm, DN_HEADS * d + h0), colb(sm, DN_HEADS * d + h1))
            q2 = stack2(q_ref[0, rows, hs(h0)], q_ref[0, rows, hs(h1)])
            k2 = stack2(k_ref[0, rows, hs(h0)], k_ref[0, rows, hs(h1)])
            v2 = stack2(v_ref[0, rows, hs(h0)], v_ref[0, rows, hs(h1)])
            m_incl, m_strict = masks[d]
            e = jnp.exp((g2 - g2.T) * m_incl)
            eg = jnp.exp(g2)
            end_row = CHUNK - 1 if d == 0 else 0
            g_end = stack2(jnp.broadcast_to(g2[end_row:end_row + 1, :], (CHUNK, LANES)),
                           jnp.broadcast_to(g2[CHUNK + end_row:CHUNK + end_row + 1, :], (CHUNK, LANES)))
            st[key] = dict(
                k2b=k2.astype(BF16), q2b=q2.astype(BF16), b2=b2,
                dec_strict=e * m_strict, dec_incl=e * m_incl,
                rhs=jnp.concatenate([v2 * b2, k2 * (b2 * eg)], axis=1).astype(BF16),
                qd=q2 * eg, k_end=(k2 * jnp.exp(g_end - g2)).astype(BF16),
                decay=[jnp.exp(g2[hh * CHUNK + end_row: hh * CHUNK + end_row + 1, :]) for hh in (0, 1)],
            )

    def gram(keys):
        for key in keys:
            c_ = st[key]
            c_["a"] = _dot_nt(c_["k2b"], c_["k2b"]) * c_["b2"] * c_["dec_strict"]
            c_["qk"] = (_dot_nt(c_["q2b"], c_["k2b"]) * c_["dec_incl"]).astype(BF16)
            c_["pw"] = c_["a"] * blk_diag
            c_["t"] = eye - c_["pw"]
            c_["pwb"] = c_["pw"].astype(BF16)

    def square(keys):
        for key in keys:
            st[key]["pwb"] = _dot(st[key]["pwb"], st[key]["pwb"]).astype(BF16)

    def accumulate(keys):
        for key in keys:
            st[key]["t"] = st[key]["t"] + _dot(st[key]["t"].astype(BF16), st[key]["pwb"])

    def merge(keys):
        for key in keys:
            st[key]["t"] = st[key]["t"] - _dot(st[key]["tb"], st[key]["pwb"])

    def solve(keys):
        for key in keys:
            st[key]["uw"] = _dot(st[key]["t"].astype(BF16), st[key]["rhs"])

    stages = [operands, gram]
    lvl = 2
    while lvl < INV_BASE:
        stages += [square, accumulate]
        lvl *= 2
    for m_off in merge_masks:
        def off_times_t(keys, m_off=m_off):
            for key in keys:
                st[key]["tb"] = st[key]["t"].astype(BF16)
                st[key]["pwb"] = _dot((st[key]["a"] * m_off).astype(BF16), st[key]["tb"]).astype(BF16)

        stages += [off_times_t, merge]
    stages.append(solve)

    state = {(d, hd): s_scr[d, hd] for d in (0, 1) for hd in range(DN_HEADS)}
    ws, vn = {}, {}

    def step_keys(step):
        return [(d, step if d == 0 else n_chunks - 1 - step, p) for d in (0, 1) for p in range(n_pairs)]

    def apply_state(step):
        for key in step_keys(step):
            d, c, p = key
            c_ = st[key]
            for hh in (0, 1):
                sl = slice(hh * CHUNK, (hh + 1) * CHUNK)
                lhs = stack2(c_["uw"][sl, DV:], c_["qd"][sl]).astype(BF16)
                ws[key, hh] = _dot(lhs, state[d, 2 * p + hh].astype(BF16))

    def emit_output(step):
        for key in step_keys(step):
            d, c, p = key
            c_ = st[key]
            vn[key] = stack2(*[c_["uw"][hh * CHUNK:(hh + 1) * CHUNK, :DV] - ws[key, hh][:CHUNK]
                               for hh in (0, 1)]).astype(BF16)
            o2 = stack2(ws[key, 0][CHUNK:], ws[key, 1][CHUNK:]) + _dot(c_["qk"], vn[key])
            o_ref = of_ref if d == 0 else ob_ref
            for hh in (0, 1):
                o_ref[0, c * CHUNK:(c + 1) * CHUNK, hs(2 * p + hh)] = (
                    o2[hh * CHUNK:(hh + 1) * CHUNK].astype(o_ref.dtype))

    def update_state(step):
        for key in step_keys(step):
            d, c, p = key
            c_ = st[key]
            for hh in (0, 1):
                sl = slice(hh * CHUNK, (hh + 1) * CHUNK)
                state[d, 2 * p + hh] = (state[d, 2 * p + hh] * c_["decay"][hh]
                                        + _dot_tn(c_["k_end"][sl], vn[key][sl]))

    def sub_steps(steps):
        return [functools.partial(f, step) for step in steps for f in (apply_state, emit_output, update_state)]

    for stage in stages:
        stage(chains)
    for sub in sub_steps(range(n_chunks)):
        sub()
    for (d, hd), val in state.items():
        s_scr[d, hd] = val

    if want_state:
        @pl.when(j == nblk - 1)
        def _():
            so_ref[0] = s_scr[...]


def _gdn(q, k, v, small, s0, layer, want_state, name):
    b, l, _ = q.shape
    nblk = l // TM
    fwd = lambda bi, j: (bi, j, 0)
    bwd = lambda bi, j: (bi, nblk - 1 - j, 0)
    st = lambda bi, j: (bi, 0, 0, 0, 0)
    in_specs, args = [], []
    for imap in (fwd, bwd):
        in_specs += [pl.BlockSpec((1, TM, QK_W), imap)] * 3 + [pl.BlockSpec((1, TM, LANES), imap)]
        args += [q, k, v, small]
    if s0 is not None:
        in_specs.append(pl.BlockSpec((1, 1, 2, DN_HEADS, DK, DV), lambda bi, j: (bi, layer, 0, 0, 0, 0)))
        args.append(s0)
    out_specs = [pl.BlockSpec((1, TM, V_W), fwd), pl.BlockSpec((1, TM, V_W), bwd)]
    out_shape = [jax.ShapeDtypeStruct((b, l, V_W), BF16)] * 2
    if want_state:
        out_specs.append(pl.BlockSpec((1, 2, DN_HEADS, DK, DV), st))
        out_shape.append(jax.ShapeDtypeStruct((b, 2, DN_HEADS, DK, DV), F32))
    return pl.pallas_call(
        functools.partial(_gdn_kernel, s0 is not None, want_state, nblk),
        grid=(b, nblk),
        in_specs=in_specs,
        out_specs=out_specs,
        out_shape=out_shape,
        scratch_shapes=[
            pltpu.VMEM((2, DN_HEADS, DK, DV), F32),
            pltpu.VMEM((TM, LANES), F32),
            pltpu.VMEM((TM, LANES), F32),
        ],
        compiler_params=_params("arbitrary", "arbitrary"),
        name=name,
    )(*args)


def _post_kernel(tiles_per_batch, row0, x_ref, of_ref, ob_ref, g_ref, yc_ref, mod_ref, onw_ref,
                 wo_ref, n2_ref, xo_ref, h2_ref):
    row = row0 + pl.program_id(0) // tiles_per_batch if tiles_per_batch else row0
    o = of_ref[...].astype(F32) + ob_ref[...].astype(F32)
    gated = []
    for hd in range(DN_HEADS):
        oh = o[:, hd * DV:(hd + 1) * DV]
        on = oh * lax.rsqrt(_mean_sq(oh) + EPS) * onw_ref[0]
        gated.append((on * g_ref[:, hd * DV:(hd + 1) * DV]).astype(BF16))
    og = jnp.concatenate(gated, axis=1)
    y = _dot(yc_ref[...], wo_ref[0, 0:CONV_W, :]) + _dot(og, wo_ref[0, CONV_W:, :])
    x = x_ref[...] + _mod_row(mod_ref, 2, row) * y
    xo_ref[...] = x
    xn = x * lax.rsqrt(_mean_sq(x) + EPS)
    h2 = xn * (n2_ref[0] * (1.0 + _mod_row(mod_ref, 4, row))) + _mod_row(mod_ref, 3, row)
    h2_ref[...] = h2.astype(BF16)


def _post_mixer(layer, x, o_f, o_b, gate, yconv, mod, tiles_per_batch, row0, w, tag):
    t = x.shape[0]
    tile = lambda i: (i, 0)
    lay3 = lambda i: (layer, 0, 0)
    return pl.pallas_call(
        functools.partial(_post_kernel, tiles_per_batch, row0),
        grid=(t // TM_POST,),
        in_specs=[
            pl.BlockSpec((TM_POST, D_MODEL), tile),
            pl.BlockSpec((TM_POST, V_W), tile),
            pl.BlockSpec((TM_POST, V_W), tile),
            pl.BlockSpec((TM_POST, V_W), tile),
            pl.BlockSpec((TM_POST, CONV_W), tile),
            pl.BlockSpec((1, 6, MOD_ROWS, D_MODEL), lambda i: (layer, 0, 0, 0)),
            pl.BlockSpec((1, 1, DV), lay3),
            pl.BlockSpec((1, D_MODEL, D_MODEL), lay3),
            pl.BlockSpec((1, 1, D_MODEL), lay3),
        ],
        out_specs=[pl.BlockSpec((TM_POST, D_MODEL), tile), pl.BlockSpec((TM_POST, D_MODEL), tile)],
        out_shape=[jax.ShapeDtypeStruct((t, D_MODEL), F32), jax.ShapeDtypeStruct((t, D_MODEL), BF16)],
        compiler_params=_params("arbitrary"),
        name=f"post_mixer_l{layer}_{tag}",
    )(x, o_f, o_b, gate, yconv, mod, w["o_norm_w"], w["w_out"], w["norm2_w"])


def _ffn_kernel(row_len, multi_row, block_rows, row0, final, h_ref, x_ref, mod_ref, fw_ref, wu_ref,
                wg_ref, cw_ref, wd_ref, o_ref, u_scr, g_scr, a_scr):
    part = TB // FFN_PARTS
    n_sub = part // CONV_ROWS
    n_slab = FB // LANES

    @pl.when(pl.program_id(1) == 0)
    def _():
        o_ref[...] = jnp.zeros_like(o_ref)
        for s in range(n_slab):
            g_scr[s, 0:HALO, :] = jnp.zeros((HALO, LANES), F32)
            g_scr[s, HALO + TB:2 * HALO + TB, :] = jnp.zeros((HALO, LANES), F32)

    def up_proj(p):
        rows = slice(p * part, (p + 1) * part)
        hb = h_ref[rows, :]
        u = _dot(hb, wu_ref[0])
        g = _dot(hb, wg_ref[0])
        for s in range(n_slab):
            u_scr[s, rows, :] = u[:, s * LANES:(s + 1) * LANES].astype(BF16)
            g_scr[s, HALO + p * part:HALO + (p + 1) * part, :] = g[:, s * LANES:(s + 1) * LANES]

    tok = lax.broadcasted_iota(jnp.int32, (CONV_ROWS, 1), 0)

    def taps(r0, s):
        base = r0 + HALO
        g = g_scr[s, base:base + CONV_ROWS, :]
        gm = g_scr[s, base - 1:base - 1 + CONV_ROWS, :]
        gp = g_scr[s, base + 1:base + 1 + CONV_ROWS, :]
        if r0 % row_len == 0:
            gm = jnp.where(tok == 0, 0.0, gm)
        if (r0 + CONV_ROWS) % row_len == 0:
            gp = jnp.where(tok == CONV_ROWS - 1, 0.0, gp)
        return gm.astype(BF16), g.astype(BF16), gp.astype(BF16)

    def conv_gate(p):
        for s in range(n_slab):
            lanes = slice(s * LANES, (s + 1) * LANES)
            cw = [cw_ref[0, i:i + 1, lanes].astype(BF16) for i in range(9)]
            cache = {}
            for sb in range(p * n_sub, (p + 1) * n_sub):
                r0 = sb * CONV_ROWS
                conv = None
                for ky in (0, 1, 2) if multi_row else (1,):
                    src = r0 + (ky - 1) * row_len
                    if src < 0 or src >= TB:
                        continue
                    if src not in cache:
                        cache[src] = taps(src, s)
                    gm, g, gp = cache[src]
                    term = cw[3 * ky] * gm + cw[3 * ky + 1] * g + cw[3 * ky + 2] * gp
                    conv = term if conv is None else conv + term
                act = _gelu_tanh_bf16(conv) * u_scr[s, r0:r0 + CONV_ROWS, :]
                a_scr[s, r0:r0 + CONV_ROWS, :] = act

    def down_proj(p):
        rows = slice(p * part, (p + 1) * part)
        act = jnp.concatenate([a_scr[s, rows, :] for s in range(n_slab)], axis=1)
        o_ref[rows, :] += _dot(act, wd_ref[0])

    up_proj(0)
    for p in range(FFN_PARTS):
        if p + 1 < FFN_PARTS:
            up_proj(p + 1)
        if p > 0:
            down_proj(p - 1)
        conv_gate(p)
    down_proj(FFN_PARTS - 1)

    @pl.when(pl.program_id(1) == pl.num_programs(1) - 1)
    def _():
        row = row0 + pl.program_id(0) if block_rows else row0
        gate2 = _mod_row(mod_ref, 5, row)

        def finish(c, carry):
            rows = pl.ds(pl.multiple_of(c * TM, TM), TM)
            x = x_ref[rows, :] + gate2 * o_ref[rows, :]
            if final:
                x = x * lax.rsqrt(_mean_sq(x) + EPS) * fw_ref[...]
            o_ref[rows, :] = x
            return carry

        lax.fori_loop(0, TB // TM, finish, 0)


def _conv_glu(layer, h2, x, mod, row_len, multi_row, block_rows, row0, w, tag):
    t = h2.shape[0]
    nf = D_FF // FB
    return pl.pallas_call(
        functools.partial(_ffn_kernel, row_len, multi_row, block_rows, row0, layer == DEPTH - 1),
        grid=(t // TB, nf),
        in_specs=[
            pl.BlockSpec((TB, D_MODEL), lambda i, f: (i, 0)),
            pl.BlockSpec((TB, D_MODEL), lambda i, f: (i, 0)),
            pl.BlockSpec((1, 6, MOD_ROWS, D_MODEL), lambda i, f: (layer, 0, 0, 0)),
            pl.BlockSpec((1, D_MODEL), lambda i, f: (0, 0)),
            pl.BlockSpec((1, D_MODEL, FB), lambda i, f: (layer, 0, f)),
            pl.BlockSpec((1, D_MODEL, FB), lambda i, f: (layer, 0, nf + f)),
            pl.BlockSpec((1, 9, FB), lambda i, f: (layer, 0, f)),
            pl.BlockSpec((1, FB, D_MODEL), lambda i, f: (layer, f, 0)),
        ],
        out_specs=pl.BlockSpec((TB, D_MODEL), lambda i, f: (i, 0)),
        out_shape=jax.ShapeDtypeStruct((t, D_MODEL), F32),
        scratch_shapes=[
            pltpu.VMEM((FB // LANES, TB, LANES), BF16),
            pltpu.VMEM((FB // LANES, TB + 2 * HALO, LANES), F32),
            pltpu.VMEM((FB // LANES, TB, LANES), BF16),
        ],
        compiler_params=_params("arbitrary", "arbitrary"),
        name=f"conv_glu_l{layer}_{tag}",
    )(h2, x, mod, w["final_norm_w"], w["w_up"], w["w_up"], w["ffn_conv_w"], w["w_down"])


def kernel(x_prompt, x_sample, state_deltanet, c, c_ctx, w_ada, b_ada, norm1_w, w_in, conv_a_w,
           qkv_conv_w, a_log, dt_bias, o_norm_w, w_out, norm2_w, w_up, ffn_conv_w, w_down,
           final_norm_w):
    b_ctx, ctx_len, _ = x_prompt.shape
    b_lat, lat_len, _ = x_sample.shape
    assert ctx_len == TM and lat_len == TB and lat_len % GRID_W == 0 and b_lat + 1 <= MOD_ROWS
    assert (b_ctx * ctx_len) % TB == 0

    c_all = jnp.concatenate(
        [c_ctx[None, :], c, jnp.zeros((MOD_ROWS - 1 - b_lat, D_MODEL), F32)], axis=0)
    mod = _modulation(c_all, w_ada, b_ada)

    gate_rows = jnp.stack([a_log.reshape(DEPTH, -1), dt_bias.reshape(DEPTH, -1)], axis=1)
    gate_params = jnp.pad(gate_rows, ((0, 0), (0, 6), (2 * DN_HEADS, LANES - N_GATE_COLS)))
    w = {
        "norm1_w": norm1_w.reshape(DEPTH, 1, D_MODEL),
        "w_main": w_in[:, :, :MAIN_COLS].astype(BF16),
        "w_small": jnp.pad(w_in[:, :, MAIN_COLS:], ((0, 0), (0, 0), (0, LANES - N_GATE_COLS))).astype(BF16),
        "conv_a_w": conv_a_w,
        "qkv_conv_w": qkv_conv_w,
        "gate_params": gate_params,
        "o_norm_w": o_norm_w.reshape(DEPTH, 1, DV),
        "w_out": w_out.astype(BF16),
        "norm2_w": norm2_w.reshape(DEPTH, 1, D_MODEL),
        "w_up": w_up.astype(BF16),
        "ffn_conv_w": ffn_conv_w.reshape(DEPTH, 9, D_FF),
        "w_down": w_down.astype(BF16),
        "final_norm_w": final_norm_w.reshape(1, D_MODEL),
    }

    streams = {
        "ctx": dict(x=x_prompt.reshape(b_ctx * ctx_len, D_MODEL), b=b_ctx, l=ctx_len, row_len=ctx_len,
                    row0=0),
        "lat": dict(x=x_sample.reshape(b_lat * lat_len, D_MODEL), b=b_lat, l=lat_len, row_len=GRID_W,
                    row0=1),
    }
    ctx_states = []
    for layer in range(DEPTH):
        for tag, s in streams.items():
            tpb = lambda tile: s["l"] // tile if tag == "lat" else 0
            yconv, q, k, v, gate, small = _pre_mixer(layer, s["x"], mod, s["row_len"], tpb(TM_PRE),
                                                     s["row0"], w)
            shp = lambda a: a.reshape(s["b"], s["l"], a.shape[-1])
            if tag == "ctx":
                o_f, o_b, s_fin = _gdn(shp(q), shp(k), shp(v), shp(small), None, layer, True,
                                       f"gdn_l{layer}_ctx")
                ctx_states.append(s_fin)
            else:
                o_f, o_b = _gdn(shp(q), shp(k), shp(v), shp(small), state_deltanet, layer, False,
                                f"gdn_l{layer}_lat")
            flat = lambda a: a.reshape(s["b"] * s["l"], a.shape[-1])
            x_mid, h2 = _post_mixer(layer, s["x"], flat(o_f), flat(o_b), gate, yconv, mod,
                                    tpb(TM_POST), s["row0"], w, tag)
            s["x"] = _conv_glu(layer, h2, x_mid, mod, s["row_len"], s["l"] // s["row_len"] > 1,
                               tag == "lat", s["row0"], w, tag)
    outs = {tag: s["x"].reshape(s["b"], s["l"], D_MODEL) for tag, s in streams.items()}
    new_state = jnp.stack(ctx_states, axis=1)
    return (outs["ctx"], outs["lat"], new_state)
```

```python
import functools

import jax
import jax.numpy as jnp
from jax import lax
from jax.experimental import pallas as pl
from jax.experimental.pallas import tpu as pltpu

D_MODEL = 1024
DEPTH = 2
GRID_W = 64
CONV_W = D_MODEL // 2
DN_HEADS = 4
DK = 128
DV = 128
QK_W = DN_HEADS * DK
V_W = DN_HEADS * DV
D_FF = 2816
CHUNK = 64
EPS = 1e-6
MAIN_COLS = 3 * CONV_W + 2 * QK_W + 2 * V_W
N_GATE_COLS = 4 * DN_HEADS

LANES = 128
TM = 256
TM_PRE = 512
TM_POST = 1024
TB = 2048
FB = 256
FFN_PARTS = 2
HALO = 8
CONV_ROWS = 64
INV_BASE = 16
MOD_ROWS = 16
VMEM_LIMIT = 56 * 1024 * 1024

BF16 = jnp.bfloat16
F32 = jnp.float32


def _dot(a, b):
    return jnp.dot(a, b, preferred_element_type=F32)


def _dot_nt(a, b):
    return lax.dot_general(a, b, (((1,), (1,)), ((), ())), preferred_element_type=F32)


def _dot_tn(a, b):
    return lax.dot_general(a, b, (((0,), (0,)), ((), ())), preferred_element_type=F32)


def _silu(x):
    return x * jax.nn.sigmoid(x)


def _gelu_tanh_bf16(x):
    c = (2.0 / jnp.pi) ** 0.5
    t = ((x * x).astype(F32) * (c * 0.044715) + c).astype(BF16)
    return (x * 0.5) * (1.0 + jnp.tanh(x * t))


def _mean_sq(x):
    return jnp.mean(x * x, axis=-1, keepdims=True)


def _params(*sem):
    return pltpu.CompilerParams(dimension_semantics=sem, vmem_limit_bytes=VMEM_LIMIT)


def _mod_kernel(c_ref, w_ref, b_ref, o_ref):
    a = _silu(c_ref[...]).astype(BF16)
    o_ref[0, 0] = _dot(a, w_ref[0].astype(BF16)) + b_ref[0]


def _modulation(c_all, w_ada, b_ada):
    return pl.pallas_call(
        _mod_kernel,
        grid=(DEPTH, 6),
        in_specs=[
            pl.BlockSpec((MOD_ROWS, D_MODEL), lambda l, n: (0, 0)),
            pl.BlockSpec((1, D_MODEL, D_MODEL), lambda l, n: (l, 0, n)),
            pl.BlockSpec((1, 1, D_MODEL), lambda l, n: (l, 0, n)),
        ],
        out_specs=pl.BlockSpec((1, 1, MOD_ROWS, D_MODEL), lambda l, n: (l, n, 0, 0)),
        out_shape=jax.ShapeDtypeStruct((DEPTH, 6, MOD_ROWS, D_MODEL), F32),
        compiler_params=_params("arbitrary", "arbitrary"),
        name="modulation",
    )(c_all, w_ada, b_ada.reshape(DEPTH, 1, 6 * D_MODEL))


def _mod_row(mod_ref, which, row):
    return mod_ref[0, which, pl.ds(row, 1), :]


def _conv3(u, w_ref, lo, row_len, scr):
    n = u.shape[0]
    width = u.shape[1]
    pitch = row_len + HALO
    starts = [HALO + r * pitch for r in range(n // row_len)]
    for r, s0 in enumerate(starts):
        scr[s0:s0 + row_len, :] = u[r * row_len:(r + 1) * row_len]
    up = jnp.concatenate([scr[s0 - 1:s0 - 1 + row_len, :] for s0 in starts], axis=0)
    un = jnp.concatenate([scr[s0 + 1:s0 + 1 + row_len, :] for s0 in starts], axis=0)
    w0 = w_ref[0, 0:1, lo:lo + width]
    w1 = w_ref[0, 1:2, lo:lo + width]
    w2 = w_ref[0, 2:3, lo:lo + width]
    return w0 * up + w1 * u + w2 * un


def _pre_kernel(row_len, tiles_per_batch, row0, x_ref, mod_ref, n1_ref, wm_ref, ws_ref, cw_ref,
                qw_ref, gp_ref, yc_ref, q_ref, k_ref, v_ref, g_ref, sm_ref, conv_scr):
    row = row0 + pl.program_id(0) // tiles_per_batch if tiles_per_batch else row0

    @pl.when(pl.program_id(0) == 0)
    def _():
        conv_scr[...] = jnp.zeros_like(conv_scr)

    x = x_ref[...]
    xn = x * lax.rsqrt(_mean_sq(x) + EPS)
    h = xn * (n1_ref[0] * (1.0 + _mod_row(mod_ref, 1, row))) + _mod_row(mod_ref, 0, row)
    hb = h.astype(BF16)

    cb = _dot(hb, wm_ref[0, :, 0:CONV_W])
    cc = _dot(hb, wm_ref[0, :, CONV_W:2 * CONV_W])
    cx = _dot(hb, wm_ref[0, :, 2 * CONV_W:3 * CONV_W])
    yc_ref[...] = (cb * _conv3(cc * cx, cw_ref, 0, row_len, conv_scr.at[0])).astype(BF16)

    base = 3 * CONV_W
    for idx, out_ref in enumerate((q_ref, k_ref, v_ref)):
        z = _dot(hb, wm_ref[0, :, base + idx * QK_W: base + (idx + 1) * QK_W])
        a = _silu(_conv3(z, qw_ref, idx * QK_W, row_len, conv_scr.at[idx + 1]))
        if idx < 2:
            scale = DK ** -0.5 if idx == 0 else 1.0
            for hd in range(DN_HEADS):
                ah = a[:, hd * DK:(hd + 1) * DK]
                ss = jnp.sum(ah * ah, axis=-1, keepdims=True)
                out_ref[:, hd * DK:(hd + 1) * DK] = ah * (lax.rsqrt(ss + EPS) * scale)
        else:
            out_ref[...] = a
    g_ref[...] = _silu(_dot(hb, wm_ref[0, :, base + 3 * QK_W: base + 3 * QK_W + V_W])).astype(BF16)

    zs = _dot(hb, ws_ref[0])
    lane = lax.broadcasted_iota(jnp.int32, (TM_PRE, LANES), 1)
    a_log = gp_ref[0, 0:1, :]
    dt_bias = gp_ref[0, 1:2, :]
    log_alpha = -jnp.exp(a_log) * jax.nn.softplus(zs + dt_bias)
    sm_ref[...] = jnp.where(lane < 2 * DN_HEADS, jax.nn.sigmoid(zs),
                            jnp.where(lane < N_GATE_COLS, log_alpha, 0.0))


def _pre_mixer(layer, x, mod, row_len, tiles_per_batch, row0, w):
    t = x.shape[0]
    tile = lambda i: (i, 0)
    lay3 = lambda i: (layer, 0, 0)
    in_specs = [
        pl.BlockSpec((TM_PRE, D_MODEL), tile),
        pl.BlockSpec((1, 6, MOD_ROWS, D_MODEL), lambda i: (layer, 0, 0, 0)),
        pl.BlockSpec((1, 1, D_MODEL), lay3),
        pl.BlockSpec((1, D_MODEL, MAIN_COLS), lay3),
        pl.BlockSpec((1, D_MODEL, LANES), lay3),
        pl.BlockSpec((1, 3, CONV_W), lay3),
        pl.BlockSpec((1, 3, 2 * QK_W + V_W), lay3),
        pl.BlockSpec((1, 8, LANES), lay3),
    ]
    args = [x, mod, w["norm1_w"], w["w_main"], w["w_small"], w["conv_a_w"], w["qkv_conv_w"],
            w["gate_params"]]
    out_specs = [pl.BlockSpec((TM_PRE, CONV_W), tile)]
    out_shape = [jax.ShapeDtypeStruct((t, CONV_W), BF16)]
    for dtype in (F32, F32, F32, BF16):
        out_specs.append(pl.BlockSpec((TM_PRE, QK_W), tile))
        out_shape.append(jax.ShapeDtypeStruct((t, QK_W), dtype))
    out_specs.append(pl.BlockSpec((TM_PRE, LANES), tile))
    out_shape.append(jax.ShapeDtypeStruct((t, LANES), F32))
    return pl.pallas_call(
        functools.partial(_pre_kernel, row_len, tiles_per_batch, row0),
        grid=(t // TM_PRE,),
        in_specs=in_specs,
        out_specs=out_specs,
        out_shape=out_shape,
        scratch_shapes=[pltpu.VMEM((4, HALO + (TM_PRE // row_len) * (row_len + HALO), CONV_W), F32)],
        compiler_params=_params("arbitrary"),
        name=f"pre_mixer_l{layer}_r{row_len}",
    )(*args)


def _gdn_kernel(has_s0, want_state, nblk, *refs):
    refs = list(refs)
    fwd_in = refs[0:4]
    bwd_in = refs[4:8]
    pos = 8
    s0_ref = None
    if has_s0:
        s0_ref = refs[pos]
        pos += 1
    of_ref, ob_ref = refs[pos], refs[pos + 1]
    pos += 2
    so_ref = None
    if want_state:
        so_ref = refs[pos]
        pos += 1
    s_scr, gf_scr, gb_scr = refs[pos:pos + 3]

    j = pl.program_id(1)

    @pl.when(j == 0)
    def _():
        if has_s0:
            s_scr[...] = s0_ref[0, 0]
        else:
            s_scr[...] = jnp.zeros_like(s_scr)

    pc = lax.broadcasted_iota(jnp.int32, (TM, 1), 0) & (CHUNK - 1)
    g = fwd_in[3][0]
    for s in (1, 2, 4, 8, 16, 32):
        g = g + jnp.where(pc >= s, pltpu.roll(g, s, 0), 0.0)
    gf_scr[...] = g
    g = bwd_in[3][0]
    for s in (1, 2, 4, 8, 16, 32):
        g = g + jnp.where(pc < CHUNK - s, pltpu.roll(g, TM - s, 0), 0.0)
    gb_scr[...] = g

    ii = lax.broadcasted_iota(jnp.int32, (2 * CHUNK, 2 * CHUNK), 0)
    jj = lax.broadcasted_iota(jnp.int32, (2 * CHUNK, 2 * CHUNK), 1)
    blk = lambda n: (ii // n) == (jj // n)
    same = blk(CHUNK)
    eye = (ii == jj).astype(F32)
    masks = (
        ((same & (ii >= jj)).astype(F32), (same & (ii > jj)).astype(F32)),
        ((same & (ii <= jj)).astype(F32), (same & (ii < jj)).astype(F32)),
    )
    blk_diag = blk(INV_BASE).astype(F32)
    merge_masks = []
    n = INV_BASE
    while n < CHUNK:
        merge_masks.append((blk(2 * n) & jnp.logical_not(blk(n))).astype(F32))
        n *= 2

    def colb(arr, lane):
        return jnp.broadcast_to(arr[:, lane:lane + 1], (CHUNK, LANES))

    def stack2(a, b):
        return jnp.concatenate([a, b], axis=0)

    n_chunks = TM // CHUNK
    n_pairs = DN_HEADS // 2
    hs = lambda hd: slice(hd * DK, (hd + 1) * DK)
    chains = [(d, c, p) for d in (0, 1) for c in range(n_chunks) for p in range(n_pairs)]
    st = {}

    def operands(keys):
        for key in keys:
            d, c, p = key
            q_ref, k_ref, v_ref, sm_ref = fwd_in if d == 0 else bwd_in
            g_scr = gf_scr if d == 0 else gb_scr
            rows = slice(c * CHUNK, (c + 1) * CHUNK)
            h0, h1 = 2 * p, 2 * p + 1
            gc = g_scr[rows, :]
            sm = sm_ref[0, rows, :]
            g2 = stack2(colb(gc, 2 * DN_HEADS + DN_HEADS * d + h0),
                        colb(gc, 2 * DN_HEADS + DN_HEADS * d + h1))
            b2 = stack2(colb(sm, DN_HEADS * d + h0), colb(sm, DN_HEADS * d + h1))
            q2 = stack2(q_ref[0, rows, hs(h0)], q_ref[0, rows, hs(h1)])
            k2 = stack2(k_ref[0, rows, hs(h0)], k_ref[0, rows, hs(h1)])
            v2 = stack2(v_ref[0, rows, hs(h0)], v_ref[0, rows, hs(h1)])
            m_incl, m_strict = masks[d]
            e = jnp.exp((g2 - g2.T) * m_incl)
            eg = jnp.exp(g2)
            end_row = CHUNK - 1 if d == 0 else 0
            g_end = stack2(jnp.broadcast_to(g2[end_row:end_row + 1, :], (CHUNK, LANES)),
                           jnp.broadcast_to(g2[CHUNK + end_row:CHUNK + end_row + 1, :], (CHUNK, LANES)))
            st[key] = dict(
                k2b=k2.astype(BF16), q2b=q2.astype(BF16), b2=b2,
                dec_strict=e * m_strict, dec_incl=e * m_incl,
                rhs=jnp.concatenate([v2 * b2, k2 * (b2 * eg)], axis=1).astype(BF16),
                qd=q2 * eg, k_end=(k2 * jnp.exp(g_end - g2)).astype(BF16),
                decay=[jnp.exp(g2[hh * CHUNK + end_row: hh * CHUNK + end_row + 1, :]) for hh in (0, 1)],
            )

    def gram(keys):
        for key in keys:
            c_ = st[key]
            c_["a"] = _dot_nt(c_["k2b"], c_["k2b"]) * c_["b2"] * c_["dec_strict"]
            c_["qk"] = (_dot_nt(c_["q2b"], c_["k2b"]) * c_["dec_incl"]).astype(BF16)
            c_["pw"] = c_["a"] * blk_diag
            c_["t"] = eye - c_["pw"]
            c_["pwb"] = c_["pw"].astype(BF16)

    def square(keys):
        for key in keys:
            st[key]["pwb"] = _dot(st[key]["pwb"], st[key]["pwb"]).astype(BF16)

    def accumulate(keys):
        for key in keys:
            st[key]["t"] = st[key]["t"] + _dot(st[key]["t"].astype(BF16), st[key]["pwb"])

    def merge(keys):
        for key in keys:
            st[key]["t"] = st[key]["t"] - _dot(st[key]["tb"], st[key]["pwb"])

    def solve(keys):
        for key in keys:
            st[key]["uw"] = _dot(st[key]["t"].astype(BF16), st[key]["rhs"])

    stages = [operands, gram]
    lvl = 2
    while lvl < INV_BASE:
        stages += [square, accumulate]
        lvl *= 2
    for m_off in merge_masks:
        def off_times_t(keys, m_off=m_off):
            for key in keys:
                st[key]["tb"] = st[key]["t"].astype(BF16)
                st[key]["pwb"] = _dot((st[key]["a"] * m_off).astype(BF16), st[key]["tb"]).astype(BF16)

        stages += [off_times_t, merge]
    stages.append(solve)

    state = {(d, hd): s_scr[d, hd] for d in (0, 1) for hd in range(DN_HEADS)}
    ws, vn = {}, {}

    def step_keys(step):
        return [(d, step if d == 0 else n_chunks - 1 - step, p) for d in (0, 1) for p in range(n_pairs)]

    def apply_state(step):
        for key in step_keys(step):
            d, c, p = key
            c_ = st[key]
            for hh in (0, 1):
                sl = slice(hh * CHUNK, (hh + 1) * CHUNK)
                lhs = stack2(c_["uw"][sl, DV:], c_["qd"][sl]).astype(BF16)
                ws[key, hh] = _dot(lhs, state[d, 2 * p + hh].astype(BF16))

    def emit_output(step):
        for key in step_keys(step):
            d, c, p = key
            c_ = st[key]
            vn[key] = stack2(*[c_["uw"][hh * CHUNK:(hh + 1) * CHUNK, :DV] - ws[key, hh][:CHUNK]
                               for hh in (0, 1)]).astype(BF16)
            o2 = stack2(ws[key, 0][CHUNK:], ws[key, 1][CHUNK:]) + _dot(c_["qk"], vn[key])
            o_ref = of_ref if d == 0 else ob_ref
            for hh in (0, 1):
                o_ref[0, c * CHUNK:(c + 1) * CHUNK, hs(2 * p + hh)] = (
                    o2[hh * CHUNK:(hh + 1) * CHUNK].astype(o_ref.dtype))

    def update_state(step):
        for key in step_keys(step):
            d, c, p = key
            c_ = st[key]
            for hh in (0, 1):
                sl = slice(hh * CHUNK, (hh + 1) * CHUNK)
                state[d, 2 * p + hh] = (state[d, 2 * p + hh] * c_["decay"][hh]
                                        + _dot_tn(c_["k_end"][sl], vn[key][sl]))

    def sub_steps(steps):
        return [functools.partial(f, step) for step in steps for f in (apply_state, emit_output, update_state)]

    for stage in stages:
        stage(chains)
    for sub in sub_steps(range(n_chunks)):
        sub()
    for (d, hd), val in state.items():
        s_scr[d, hd] = val

    if want_state:
        @pl.when(j == nblk - 1)
        def _():
            so_ref[0] = s_scr[...]


def _gdn(q, k, v, small, s0, layer, want_state, name):
    b, l, _ = q.shape
    nblk = l // TM
    fwd = lambda bi, j: (bi, j, 0)
    bwd = lambda bi, j: (bi, nblk - 1 - j, 0)
    st = lambda bi, j: (bi, 0, 0, 0, 0)
    in_specs, args = [], []
    for imap in (fwd, bwd):
        in_specs += [pl.BlockSpec((1, TM, QK_W), imap)] * 3 + [pl.BlockSpec((1, TM, LANES), imap)]
        args += [q, k, v, small]
    if s0 is not None:
        in_specs.append(pl.BlockSpec((1, 1, 2, DN_HEADS, DK, DV), lambda bi, j: (bi, layer, 0, 0, 0, 0)))
        args.append(s0)
    out_specs = [pl.BlockSpec((1, TM, V_W), fwd), pl.BlockSpec((1, TM, V_W), bwd)]
    out_shape = [jax.ShapeDtypeStruct((b, l, V_W), BF16)] * 2
    if want_state:
        out_specs.append(pl.BlockSpec((1, 2, DN_HEADS, DK, DV), st))
        out_shape.append(jax.ShapeDtypeStruct((b, 2, DN_HEADS, DK, DV), F32))
    return pl.pallas_call(
        functools.partial(_gdn_kernel, s0 is not None, want_state, nblk),
        grid=(b, nblk),
        in_specs=in_specs,
        out_specs=out_specs,
        out_shape=out_shape,
        scratch_shapes=[
            pltpu.VMEM((2, DN_HEADS, DK, DV), F32),
            pltpu.VMEM((TM, LANES), F32),
            pltpu.VMEM((TM, LANES), F32),
        ],
        compiler_params=_params("arbitrary", "arbitrary"),
        name=name,
    )(*args)


def _post_kernel(tiles_per_batch, row0, x_ref, of_ref, ob_ref, g_ref, yc_ref, mod_ref, onw_ref,
                 wo_ref, n2_ref, xo_ref, h2_ref):
    row = row0 + pl.program_id(0) // tiles_per_batch if tiles_per_batch else row0
    o = of_ref[...].astype(F32) + ob_ref[...].astype(F32)
    gated = []
    for hd in range(DN_HEADS):
        oh = o[:, hd * DV:(hd + 1) * DV]
        on = oh * lax.rsqrt(_mean_sq(oh) + EPS) * onw_ref[0]
        gated.append((on * g_ref[:, hd * DV:(hd + 1) * DV]).astype(BF16))
    og = jnp.concatenate(gated, axis=1)
    y = _dot(yc_ref[...], wo_ref[0, 0:CONV_W, :]) + _dot(og, wo_ref[0, CONV_W:, :])
    x = x_ref[...] + _mod_row(mod_ref, 2, row) * y
    xo_ref[...] = x
    xn = x * lax.rsqrt(_mean_sq(x) + EPS)
    h2 = xn * (n2_ref[0] * (1.0 + _mod_row(mod_ref, 4, row))) + _mod_row(mod_ref, 3, row)
    h2_ref[...] = h2.astype(BF16)


def _post_mixer(layer, x, o_f, o_b, gate, yconv, mod, tiles_per_batch, row0, w, tag):
    t = x.shape[0]
    tile = lambda i: (i, 0)
    lay3 = lambda i: (layer, 0, 0)
    return pl.pallas_call(
        functools.partial(_post_kernel, tiles_per_batch, row0),
        grid=(t // TM_POST,),
        in_specs=[
            pl.BlockSpec((TM_POST, D_MODEL), tile),
            pl.BlockSpec((TM_POST, V_W), tile),
            pl.BlockSpec((TM_POST, V_W), tile),
            pl.BlockSpec((TM_POST, V_W), tile),
            pl.BlockSpec((TM_POST, CONV_W), tile),
            pl.BlockSpec((1, 6, MOD_ROWS, D_MODEL), lambda i: (layer, 0, 0, 0)),
            pl.BlockSpec((1, 1, DV), lay3),
            pl.BlockSpec((1, D_MODEL, D_MODEL), lay3),
            pl.BlockSpec((1, 1, D_MODEL), lay3),
        ],
        out_specs=[pl.BlockSpec((TM_POST, D_MODEL), tile), pl.BlockSpec((TM_POST, D_MODEL), tile)],
        out_shape=[jax.ShapeDtypeStruct((t, D_MODEL), F32), jax.ShapeDtypeStruct((t, D_MODEL), BF16)],
        compiler_params=_params("arbitrary"),
        name=f"post_mixer_l{layer}_{tag}",
    )(x, o_f, o_b, gate, yconv, mod, w["o_norm_w"], w["w_out"], w["norm2_w"])


def _ffn_kernel(row_len, multi_row, block_rows, row0, final, h_ref, x_ref, mod_ref, fw_ref, wu_ref,
                wg_ref, cw_ref, wd_ref, o_ref, u_scr, g_scr, a_scr):
    part = TB // FFN_PARTS
    n_sub = part // CONV_ROWS
    n_slab = FB // LANES

    @pl.when(pl.program_id(1) == 0)
    def _():
        o_ref[...] = jnp.zeros_like(o_ref)
        for s in range(n_slab):
            g_scr[s, 0:HALO, :] = jnp.zeros((HALO, LANES), F32)
            g_scr[s, HALO + TB:2 * HALO + TB, :] = jnp.zeros((HALO, LANES), F32)

    def up_proj(p):
        rows = slice(p * part, (p + 1) * part)
        hb = h_ref[rows, :]
        u = _dot(hb, wu_ref[0])
        g = _dot(hb, wg_ref[0])
        for s in range(n_slab):
            u_scr[s, rows, :] = u[:, s * LANES:(s + 1) * LANES].astype(BF16)
            g_scr[s, HALO + p * part:HALO + (p + 1) * part, :] = g[:, s * LANES:(s + 1) * LANES]

    tok = lax.broadcasted_iota(jnp.int32, (CONV_ROWS, 1), 0)

    def taps(r0, s):
        base = r0 + HALO
        g = g_scr[s, base:base + CONV_ROWS, :]
        gm = g_scr[s, base - 1:base - 1 + CONV_ROWS, :]
        gp = g_scr[s, base + 1:base + 1 + CONV_ROWS, :]
        if r0 % row_len == 0:
            gm = jnp.where(tok == 0, 0.0, gm)
        if (r0 + CONV_ROWS) % row_len == 0:
            gp = jnp.where(tok == CONV_ROWS - 1, 0.0, gp)
        return gm.astype(BF16), g.astype(BF16), gp.astype(BF16)

    def conv_gate(p):
        for s in range(n_slab):
            lanes = slice(s * LANES, (s + 1) * LANES)
            cw = [cw_ref[0, i:i + 1, lanes].astype(BF16) for i in range(9)]
            cache = {}
            for sb in range(p * n_sub, (p + 1) * n_sub):
                r0 = sb * CONV_ROWS
                conv = None
                for ky in (0, 1, 2) if multi_row else (1,):
                    src = r0 + (ky - 1) * row_len
                    if src < 0 or src >= TB:
                        continue
                    if src not in cache:
                        cache[src] = taps(src, s)
                    gm, g, gp = cache[src]
                    term = cw[3 * ky] * gm + cw[3 * ky + 1] * g + cw[3 * ky + 2] * gp
                    conv = term if conv is None else conv + term
                act = _gelu_tanh_bf16(conv) * u_scr[s, r0:r0 + CONV_ROWS, :]
                a_scr[s, r0:r0 + CONV_ROWS, :] = act

    def down_proj(p):
        rows = slice(p * part, (p + 1) * part)
        act = jnp.concatenate([a_scr[s, rows, :] for s in range(n_slab)], axis=1)
        o_ref[rows, :] += _dot(act, wd_ref[0])

    up_proj(0)
    for p in range(FFN_PARTS):
        if p + 1 < FFN_PARTS:
            up_proj(p + 1)
        if p > 0:
            down_proj(p - 1)
        conv_gate(p)
    down_proj(FFN_PARTS - 1)

    @pl.when(pl.program_id(1) == pl.num_programs(1) - 1)
    def _():
        row = row0 + pl.program_id(0) if block_rows else row0
        gate2 = _mod_row(mod_ref, 5, row)

        def finish(c, carry):
            rows = pl.ds(pl.multiple_of(c * TM, TM), TM)
            x = x_ref[rows, :] + gate2 * o_ref[rows, :]
            if final:
                x = x * lax.rsqrt(_mean_sq(x) + EPS) * fw_ref[...]
            o_ref[rows, :] = x
            return carry

        lax.fori_loop(0, TB // TM, finish, 0)


def _conv_glu(layer, h2, x, mod, row_len, multi_row, block_rows, row0, w, tag):
    t = h2.shape[0]
    nf = D_FF // FB
    return pl.pallas_call(
        functools.partial(_ffn_kernel, row_len, multi_row, block_rows, row0, layer == DEPTH - 1),
        grid=(t // TB, nf),
        in_specs=[
            pl.BlockSpec((TB, D_MODEL), lambda i, f: (i, 0)),
            pl.BlockSpec((TB, D_MODEL), lambda i, f: (i, 0)),
            pl.BlockSpec((1, 6, MOD_ROWS, D_MODEL), lambda i, f: (layer, 0, 0, 0)),
            pl.BlockSpec((1, D_MODEL), lambda i, f: (0, 0)),
            pl.BlockSpec((1, D_MODEL, FB), lambda i, f: (layer, 0, f)),
            pl.BlockSpec((1, D_MODEL, FB), lambda i, f: (layer, 0, nf + f)),
            pl.BlockSpec((1, 9, FB), lambda i, f: (layer, 0, f)),
            pl.BlockSpec((1, FB, D_MODEL), lambda i, f: (layer, f, 0)),
        ],
        out_specs=pl.BlockSpec((TB, D_MODEL), lambda i, f: (i, 0)),
        out_shape=jax.ShapeDtypeStruct((t, D_MODEL), F32),
        scratch_shapes=[
            pltpu.VMEM((FB // LANES, TB, LANES), BF16),
            pltpu.VMEM((FB // LANES, TB + 2 * HALO, LANES), F32),
            pltpu.VMEM((FB // LANES, TB, LANES), BF16),
        ],
        compiler_params=_params("arbitrary", "arbitrary"),
        name=f"conv_glu_l{layer}_{tag}",
    )(h2, x, mod, w["final_norm_w"], w["w_up"], w["w_up"], w["ffn_conv_w"], w["w_down"])


def kernel(x_prompt, x_sample, state_deltanet, c, c_ctx, w_ada, b_ada, norm1_w, w_in, conv_a_w,
           qkv_conv_w, a_log, dt_bias, o_norm_w, w_out, norm2_w, w_up, ffn_conv_w, w_down,
           final_norm_w):
    b_ctx, ctx_len, _ = x_prompt.shape
    b_lat, lat_len, _ = x_sample.shape
    assert ctx_len == TM and lat_len == TB and lat_len % GRID_W == 0 and b_lat + 1 <= MOD_ROWS
    assert (b_ctx * ctx_len) % TB == 0

    c_all = jnp.concatenate(
        [c_ctx[None, :], c, jnp.zeros((MOD_ROWS - 1 - b_lat, D_MODEL), F32)], axis=0)
    mod = _modulation(c_all, w_ada, b_ada)

    gate_rows = jnp.stack([a_log.reshape(DEPTH, -1), dt_bias.reshape(DEPTH, -1)], axis=1)
    gate_params = jnp.pad(gate_rows, ((0, 0), (0, 6), (2 * DN_HEADS, LANES - N_GATE_COLS)))
    w = {
        "norm1_w": norm1_w.reshape(DEPTH, 1, D_MODEL),
        "w_main": w_in.astype(BF16),
        "w_small": jnp.pad(w_in[:, :, MAIN_COLS:], ((0, 0), (0, 0), (0, LANES - N_GATE_COLS))).astype(BF16),
        "conv_a_w": conv_a_w,
        "qkv_conv_w": qkv_conv_w,
        "gate_params": gate_params,
        "o_norm_w": o_norm_w.reshape(DEPTH, 1, DV),
        "w_out": w_out.astype(BF16),
        "norm2_w": norm2_w.reshape(DEPTH, 1, D_MODEL),
        "w_up": w_up.astype(BF16),
        "ffn_conv_w": ffn_conv_w.reshape(DEPTH, 9, D_FF),
        "w_down": w_down.astype(BF16),
        "final_norm_w": final_norm_w.reshape(1, D_MODEL),
    }

    streams = {
        "ctx": dict(x=x_prompt.reshape(b_ctx * ctx_len, D_MODEL), b=b_ctx, l=ctx_len, row_len=ctx_len,
                    row0=0),
        "lat": dict(x=x_sample.reshape(b_lat * lat_len, D_MODEL), b=b_lat, l=lat_len, row_len=GRID_W,
                    row0=1),
    }
    ctx_states = []
    for layer in range(DEPTH):
        for tag, s in streams.items():
            tpb = lambda tile: s["l"] // tile if tag == "lat" else 0
            yconv, q, k, v, gate, small = _pre_mixer(layer, s["x"], mod, s["row_len"], tpb(TM_PRE),
                                                     s["row0"], w)
            shp = lambda a: a.reshape(s["b"], s["l"], a.shape[-1])
            if tag == "ctx":
                o_f, o_b, s_fin = _gdn(shp(q), shp(k), shp(v), shp(small), None, layer, True,
                                       f"gdn_l{layer}_ctx")
                ctx_states.append(s_fin)
            else:
                o_f, o_b = _gdn(shp(q), shp(k), shp(v), shp(small), state_deltanet, layer, False,
                                f"gdn_l{layer}_lat")
            flat = lambda a: a.reshape(s["b"] * s["l"], a.shape[-1])
            x_mid, h2 = _post_mixer(layer, s["x"], flat(o_f), flat(o_b), gate, yconv, mod,
                                    tpb(TM_POST), s["row0"], w, tag)
            s["x"] = _conv_glu(layer, h2, x_mid, mod, s["row_len"], s["l"] // s["row_len"] > 1,
                               tag == "lat", s["row0"], w, tag)
    outs = {tag: s["x"].reshape(s["b"], s["l"], D_MODEL) for tag, s in streams.items()}
    new_state = jnp.stack(ctx_states, axis=1)
    return (outs["ctx"], outs["lat"], new_state)
```
